```python
import math
import jax
import jax.numpy as jnp
from jax import lax
import numpy as np

D_MODEL = 1024
BATCH = 4
SEQ = 4096
DEPTH = 2
DEC_BATCH = 128
DEC_SEQ = 4
PAST_LEN = 2048
PAGE_SIZE = 128

D_FF = 2816
BRANCH_W = 512
RET_HEADS = 4
RET_DK = 64
RET_DV = 128
RET_THETA = 10000.0
RET_CHUNK = 128
HG_HEADS = 4
HG_DK = 128
HG_DV = 128
HG_CHUNK = 64
DIL_GROUPS = ((128, 1), (512, 4), (2048, 16))
N_GROUPS = 3
DIL_HEADS = 8
DIL_DH = 64
ROPE_DIM = DIL_DH // 4
ROPE_THETA = 500000.0
Q_BLOCK = 128
MEM_LEN = 256
X_HEADS = 4
X_DH = D_MODEL // X_HEADS
SPLIT_SIZES = (RET_HEADS * RET_DK, RET_HEADS * RET_DK, RET_HEADS * RET_DV, RET_HEADS * RET_DV,
               HG_HEADS * HG_DK, HG_HEADS * HG_DK, HG_HEADS * HG_DV, HG_HEADS * HG_DV,
               3 * N_GROUPS * DIL_HEADS * DIL_DH, 3 * D_MODEL)
IN_COLS = sum(SPLIT_SIZES)
ALPHA = (2 * DEPTH) ** 0.25
BETA = (8 * DEPTH) ** -0.25
EPS = 1e-5
NEG = -1e30
EXP_CLIP = 80.0
F32 = jnp.float32

kernel_name = 'hybrid_retention_hgrn2_dilated_decoder_step'


def layer_norm(x, g, b):
    xf = x.astype(F32)
    xc = xf - jnp.mean(xf, -1, keepdims=True)
    var = jnp.mean(xc * xc, -1, keepdims=True)
    return (xc * lax.rsqrt(var + EPS) * g.astype(F32) + b.astype(F32)).astype(x.dtype)


def head_norm(o, g, centre):
    b, t, h, dv = o.shape
    of = o.astype(F32)
    if centre:
        of = of - jnp.mean(of, -1, keepdims=True)
    of = of * lax.rsqrt(jnp.mean(of * of, -1, keepdims=True) + EPS)
    return (of.reshape(b, t, h * dv) * g.astype(F32)).astype(o.dtype)


def rotary(x, pos, rot_dim, theta):
    half = rot_dim // 2
    inv = jnp.power(jnp.float32(theta), -jnp.arange(half, dtype=F32) / half)
    ang = pos.astype(F32)[:, None] * inv[None, :]
    cos = jnp.cos(ang)[:, None, :]
    sin = jnp.sin(ang)[:, None, :]
    xf = x[..., :rot_dim].astype(F32)
    x1, x2 = xf[..., :half], xf[..., half:]
    rot = jnp.concatenate([x1 * cos - x2 * sin, x2 * cos + x1 * sin], -1).astype(x.dtype)
    if rot_dim == x.shape[-1]:
        return rot
    return jnp.concatenate([rot, x[..., rot_dim:]], -1)


def swiglu(x, w_gate, w_up, w_down):
    hidden = jax.nn.silu(jnp.einsum('btd,df->btf', x, w_gate)) * jnp.einsum('btd,df->btf', x, w_up)
    return jnp.einsum('btf,fd->btd', hidden, w_down)


def to_chunks(a, c):
    b, t, h, d = a.shape
    return a.reshape(b, t // c, c, h, d).transpose(1, 0, 3, 2, 4)


def from_chunks(a):
    n, b, h, c, d = a.shape
    return a.transpose(1, 0, 3, 2, 4).reshape(b, n * c, h, d)


def retention(q, k, v, state):
    t, dk = q.shape[1], q.shape[3]
    c = math.gcd(t, RET_CHUNK)
    log_gamma = jnp.log1p(-jnp.exp2(-5.0 - jnp.arange(RET_HEADS, dtype=F32)))
    i = jnp.arange(c, dtype=F32)
    diff = i[:, None] - i[None, :]
    d_intra = jnp.where(diff >= 0, jnp.exp(log_gamma[:, None, None] * jnp.maximum(diff, 0.0)), 0.0)
    d_q = jnp.exp(log_gamma[:, None] * (i + 1.0))[:, :, None]
    d_k = jnp.exp(log_gamma[:, None] * (c - 1.0 - i))[:, :, None]
    d_c = jnp.exp(log_gamma * c)[:, None, None]
    qc = to_chunks(q.astype(F32) * dk ** -0.5, c)
    kc = to_chunks(k.astype(F32), c)
    vc = to_chunks(v.astype(F32), c)

    def step(r, inp):
        qi, ki, vi = inp
        s = jnp.einsum('bhtk,bhsk->bhts', qi, ki) * d_intra
        o = jnp.einsum('bhts,bhsv->bhtv', s, vi) + jnp.einsum('bhtk,bhkv->bhtv', qi * d_q, r)
        r = r * d_c + jnp.einsum('bhsk,bhsv->bhkv', ki * d_k, vi)
        return r, o

    r, o = lax.scan(step, state.astype(F32), (qc, kc, vc))
    return from_chunks(o).astype(v.dtype), r


def hgrn2(q, log_f, v, state):
    t = q.shape[1]
    c = math.gcd(t, HG_CHUNK)
    causal = jnp.tril(jnp.ones((c, c), dtype=bool))
    qc = to_chunks(q.astype(F32), c)
    gc = to_chunks(log_f.astype(F32), c)
    vc = to_chunks(v.astype(F32), c)

    def step(s_mat, inp):
        qi, gi, vi = inp
        b = jnp.cumsum(gi, axis=2)
        ki = -jnp.expm1(gi)
        o = jnp.einsum('bhtk,bhkv->bhtv', qi * jnp.exp(b), s_mat)
        rel = jnp.where(causal[:, :, None], b[:, :, :, None, :] - b[:, :, None, :, :], NEG)
        a = jnp.einsum('bhtk,bhsk,bhtsk->bhts', qi, ki, jnp.exp(rel))
        o = o + jnp.einsum('bhts,bhsv->bhtv', a, vi)
        b_end = b[:, :, -1]
        s_mat = s_mat * jnp.exp(b_end)[..., None] + jnp.einsum(
            'bhsk,bhsv->bhkv', ki * jnp.exp(b_end[:, :, None, :] - b), vi)
        return s_mat, o

    s_fin, o = lax.scan(step, state.astype(F32), (qc, gc, vc))
    return from_chunks(o).astype(v.dtype), s_fin


def dilated_attention(q, k_all, v_all, row_start, dil, n_keys):
    b, t, h, dh = q.shape
    qb = math.gcd(t, Q_BLOCK)
    nb = t // qb
    offs = jnp.arange(n_keys, dtype=jnp.int32) * dil
    q_blocks = q.reshape(b, nb, qb, h, dh).transpose(1, 0, 2, 3, 4)

    def block(args):
        bi, q_blk = args
        rows = row_start + bi * qb + jnp.arange(qb, dtype=jnp.int32)
        idx = rows[:, None] - offs[None, :]
        valid = idx >= 0
        idx = jnp.maximum(idx, 0)
        k_g = jnp.take(k_all, idx, axis=1)
        v_g = jnp.take(v_all, idx, axis=1)
        s = jnp.einsum('bqhd,bqkhd->bqhk', q_blk, k_g).astype(F32) * dh ** -0.5
        s = jnp.where(valid[None, :, None, :], s, NEG)
        m = jnp.max(s, -1, keepdims=True)
        p = jnp.exp(s - m)
        l = jnp.sum(p, -1, keepdims=True)
        o = jnp.einsum('bqhk,bqkhd->bqhd', (p / l).astype(v_all.dtype), v_g)
        return o, (m + jnp.log(l))[..., 0]

    o, lse = lax.map(block, (jnp.arange(nb, dtype=jnp.int32), q_blocks))
    return (o.transpose(1, 0, 2, 3, 4).reshape(b, t, h, dh),
            lse.transpose(1, 0, 2, 3).reshape(b, t, h))


def mixer_block(x, pos0, w_in, ret_gn_g, lb, hgrn_norm_g, w_branch, w_out,
                ret_state, hgrn_state, win_bufs):
    b, t, _ = x.shape
    pos = pos0 + jnp.arange(t, dtype=jnp.int32)
    proj = jnp.einsum('btd,de->bte', x, w_in)
    parts = []
    off = 0
    for size in SPLIT_SIZES:
        parts.append(proj[..., off:off + size])
        off += size
    a_q, a_k, a_v, a_g, b_q, b_f, b_i, b_g, c_qkv, gate_raw = parts

    qa = rotary(a_q.reshape(b, t, RET_HEADS, RET_DK), pos, RET_DK, RET_THETA)
    ka = rotary(a_k.reshape(b, t, RET_HEADS, RET_DK), pos, RET_DK, RET_THETA)
    oa, ret_state = retention(qa, ka, a_v.reshape(b, t, RET_HEADS, RET_DV), ret_state)
    oa = head_norm(oa, ret_gn_g, True) * jax.nn.silu(a_g)

    qb = (jax.nn.silu(b_q) * HG_DK ** -0.5).reshape(b, t, HG_HEADS, HG_DK)
    lbh = lb.astype(F32).reshape(HG_HEADS, HG_DK)
    zf = b_f.astype(F32).reshape(b, t, HG_HEADS, HG_DK)
    log_f = jax.nn.log_sigmoid(zf) + jnp.log1p(lbh * jnp.exp(jnp.minimum(-zf, EXP_CLIP)))
    ob, hgrn_state = hgrn2(qb, log_f, b_i.reshape(b, t, HG_HEADS, HG_DV), hgrn_state)
    ob = head_norm(ob, hgrn_norm_g, False) * jax.nn.sigmoid(b_g)

    qkv = c_qkv.reshape(b, t, 3, N_GROUPS * DIL_HEADS, DIL_DH)
    qc = rotary(qkv[:, :, 0], pos, ROPE_DIM, ROPE_THETA).reshape(b, t, N_GROUPS, DIL_HEADS, DIL_DH)
    kc = rotary(qkv[:, :, 1], pos, ROPE_DIM, ROPE_THETA).reshape(b, t, N_GROUPS, DIL_HEADS, DIL_DH)
    vc = qkv[:, :, 2].reshape(b, t, N_GROUPS, DIL_HEADS, DIL_DH)
    outs, lses, rows = [], [], []
    for g, (window, dil) in enumerate(DIL_GROUPS):
        kg, vg = kc[:, :, g], vc[:, :, g]
        if win_bufs is None:
            k_all, v_all, row_start = kg, vg, 0
        else:
            kbuf, vbuf = win_bufs[g]
            k_all = jnp.concatenate([kbuf.astype(kg.dtype), kg], axis=1)
            v_all = jnp.concatenate([vbuf.astype(vg.dtype), vg], axis=1)
            row_start = kbuf.shape[1]
        o, lse = dilated_attention(qc[:, :, g], k_all, v_all, row_start, dil, window // dil + 1)
        outs.append(o)
        lses.append(lse)
        keep = min(window, t)
        rows.append((kg[:, t - keep:], vg[:, t - keep:]))
    w_den = jax.nn.softmax(jnp.stack(lses, 0), axis=0)
    oc = jnp.sum(w_den[..., None] * jnp.stack(outs, 0).astype(F32), axis=0)
    oc = oc.astype(x.dtype).reshape(b, t, BRANCH_W)

    gates = jax.nn.sigmoid(gate_raw.reshape(b, t, 3, D_MODEL))
    branches = jnp.stack([oa, ob, oc], axis=2)
    merged = jnp.sum(gates * jnp.einsum('btgc,gcd->btgd', branches, w_branch), axis=2)
    return jnp.einsum('btd,de->bte', merged, w_out), ret_state, hgrn_state, rows


def cross_attention(x, w_q, w_o, mem_k, mem_v):
    b, t, _ = x.shape
    q = jnp.einsum('btd,de->bte', x, w_q).reshape(b, t, X_HEADS, X_DH)
    s = jnp.einsum('bthd,bmhd->bhtm', q, mem_k.astype(q.dtype)).astype(F32) * X_DH ** -0.5
    p = jax.nn.softmax(s, axis=-1).astype(x.dtype)
    o = jnp.einsum('bhtm,bmhd->bthd', p, mem_v.astype(x.dtype)).reshape(b, t, D_MODEL)
    return jnp.einsum('btd,de->bte', o, w_o)


def trunk_layer(x, pos0, ln_g, ln_b, ffn_w_gate, ffn_w_up, ffn_w_down, w_in, ret_gn_g, lb,
                hgrn_norm_g, w_branch, w_out, xattn_w_q, xattn_w_o, mem_k, mem_v,
                ret_state, hgrn_state, win_bufs):
    h = layer_norm(ALPHA * x + 0.5 * swiglu(x, ffn_w_gate[0], ffn_w_up[0], ffn_w_down[0]),
                   ln_g[0], ln_b[0])
    mix, ret_state, hgrn_state, rows = mixer_block(h, pos0, w_in, ret_gn_g, lb, hgrn_norm_g,
                                                   w_branch, w_out, ret_state, hgrn_state, win_bufs)
    h = layer_norm(ALPHA * h + mix, ln_g[1], ln_b[1])
    h = layer_norm(ALPHA * h + cross_attention(h, xattn_w_q, xattn_w_o, mem_k, mem_v), ln_g[2], ln_b[2])
    h = layer_norm(ALPHA * h + 0.5 * swiglu(h, ffn_w_gate[1], ffn_w_up[1], ffn_w_down[1]),
                   ln_g[3], ln_b[3])
    return h, ret_state, hgrn_state, rows


def hgrn_lower_bounds(raw):
    p = jax.nn.softmax(raw.astype(F32), axis=0)
    return jnp.cumsum(p, axis=0) - p[0:1]


def setup_inputs(seed: int = 0) -> dict:
    key = jax.random.key(seed)
    keys = jax.random.split(key, 32)

    def nrm(i, shape, scale=1.0):
        return jax.random.normal(keys[i], shape, F32) * scale

    win_lens = [min(w, PAST_LEN) for w, _ in DIL_GROUPS]
    wshape = lambda n: (DEPTH, DEC_BATCH, n, DIL_HEADS, DIL_DH)
    return {
        'x_prompt': nrm(0, (BATCH, SEQ, D_MODEL)),
        'x_sample': nrm(1, (DEC_BATCH, DEC_SEQ, D_MODEL)),
        'mem_prompt': nrm(2, (BATCH, MEM_LEN, D_MODEL)),
        'state_ret': nrm(3, (DEPTH, DEC_BATCH, RET_HEADS, RET_DK, RET_DV), 0.5),
        'state_hgrn': nrm(4, (DEPTH, DEC_BATCH, HG_HEADS, HG_DK, HG_DV), 0.5),
        'cache_win_k0': nrm(5, wshape(win_lens[0])),
        'cache_win_v0': nrm(6, wshape(win_lens[0])),
        'cache_win_k1': nrm(7, wshape(win_lens[1])),
        'cache_win_v1': nrm(8, wshape(win_lens[1])),
        'cache_win_k2': nrm(9, wshape(win_lens[2])),
        'cache_win_v2': nrm(10, wshape(win_lens[2])),
        'cache_mem_k': nrm(11, (DEPTH, DEC_BATCH, MEM_LEN, X_HEADS, X_DH)),
        'cache_mem_v': nrm(12, (DEPTH, DEC_BATCH, MEM_LEN, X_HEADS, X_DH)),
        'ln_g': 1.0 + nrm(13, (DEPTH, 4, D_MODEL), 0.02),
        'ln_b': nrm(14, (DEPTH, 4, D_MODEL), 0.02),
        'ffn_w_gate': nrm(15, (DEPTH, 2, D_MODEL, D_FF), D_MODEL ** -0.5),
        'ffn_w_up': nrm(16, (DEPTH, 2, D_MODEL, D_FF), D_MODEL ** -0.5),
        'ffn_w_down': nrm(17, (DEPTH, 2, D_FF, D_MODEL), BETA * D_FF ** -0.5),
        'w_in': nrm(18, (DEPTH, D_MODEL, IN_COLS), D_MODEL ** -0.5),
        'ret_gn_g': 1.0 + nrm(19, (DEPTH, BRANCH_W), 0.02),
        'hgrn_lb_raw': nrm(20, (DEPTH, HG_HEADS * HG_DK), 0.5),
        'hgrn_norm_g': 1.0 + nrm(21, (DEPTH, BRANCH_W), 0.02),
        'w_branch': nrm(22, (DEPTH, 3, BRANCH_W, D_MODEL), BETA * BRANCH_W ** -0.5),
        'w_out': nrm(23, (DEPTH, D_MODEL, D_MODEL), BETA * D_MODEL ** -0.5),
        'xattn_w_q': nrm(24, (DEPTH, D_MODEL, D_MODEL), D_MODEL ** -0.5),
        'xattn_w_k': nrm(25, (DEPTH, D_MODEL, D_MODEL), D_MODEL ** -0.5),
        'xattn_w_v': nrm(26, (DEPTH, D_MODEL, D_MODEL), D_MODEL ** -0.5),
        'xattn_w_o': nrm(27, (DEPTH, D_MODEL, D_MODEL), BETA * D_MODEL ** -0.5),
    }


def reference(x_prompt, x_sample, mem_prompt, state_ret, state_hgrn,
              cache_win_k0, cache_win_v0, cache_win_k1, cache_win_v1, cache_win_k2, cache_win_v2,
              cache_mem_k, cache_mem_v,
              ln_g, ln_b, ffn_w_gate, ffn_w_up, ffn_w_down, w_in, ret_gn_g, hgrn_lb_raw,
              hgrn_norm_g, w_branch, w_out, xattn_w_q, xattn_w_k, xattn_w_v, xattn_w_o):
    lb_all = hgrn_lower_bounds(hgrn_lb_raw)
    cache_k = (cache_win_k0, cache_win_k1, cache_win_k2)
    cache_v = (cache_win_v0, cache_win_v1, cache_win_v2)
    bp = x_prompt.shape[0]
    xp, xs = x_prompt, x_sample
    names = ['ret_p', 'hg_p', 'mk_p', 'mv_p', 'ret_s', 'hg_s']
    for g in range(N_GROUPS):
        names += ['wk%d_p' % g, 'wv%d_p' % g, 'wk%d_s' % g, 'wv%d_s' % g]
    acc = {n: [] for n in names}
    for l in range(DEPTH):
        wl = (ln_g[l], ln_b[l], ffn_w_gate[l], ffn_w_up[l], ffn_w_down[l], w_in[l], ret_gn_g[l],
              lb_all[l], hgrn_norm_g[l], w_branch[l], w_out[l], xattn_w_q[l], xattn_w_o[l])
        mk = jnp.einsum('bmd,de->bme', mem_prompt, xattn_w_k[l]).reshape(bp, MEM_LEN, X_HEADS, X_DH)
        mv = jnp.einsum('bmd,de->bme', mem_prompt, xattn_w_v[l]).reshape(bp, MEM_LEN, X_HEADS, X_DH)
        r0 = jnp.zeros((bp, RET_HEADS, RET_DK, RET_DV), F32)
        s0 = jnp.zeros((bp, HG_HEADS, HG_DK, HG_DV), F32)
        xp, rp, gp, rows_p = trunk_layer(xp, 0, *wl, mk, mv, r0, s0, None)
        bufs = tuple((cache_k[g][l], cache_v[g][l]) for g in range(N_GROUPS))
        xs, rs, gs, rows_s = trunk_layer(xs, PAST_LEN, *wl, cache_mem_k[l], cache_mem_v[l],
                                         state_ret[l], state_hgrn[l], bufs)
        acc['ret_p'].append(rp.astype(x_prompt.dtype))
        acc['hg_p'].append(gp.astype(x_prompt.dtype))
        acc['mk_p'].append(mk)
        acc['mv_p'].append(mv)
        acc['ret_s'].append(rs.astype(x_sample.dtype))
        acc['hg_s'].append(gs.astype(x_sample.dtype))
        for g in range(N_GROUPS):
            acc['wk%d_p' % g].append(rows_p[g][0])
            acc['wv%d_p' % g].append(rows_p[g][1])
            acc['wk%d_s' % g].append(rows_s[g][0])
            acc['wv%d_s' % g].append(rows_s[g][1])
    st = {n: jnp.stack(v, axis=0) for n, v in acc.items()}
    return (xp, xs,
            st['ret_p'], st['hg_p'],
            st['wk0_p'], st['wv0_p'], st['wk1_p'], st['wv1_p'], st['wk2_p'], st['wv2_p'],
            st['mk_p'], st['mv_p'],
            st['ret_s'], st['hg_s'],
            st['wk0_s'], st['wv0_s'], st['wk1_s'], st['wv1_s'], st['wk2_s'], st['wv2_s'])
```

```python
import functools
import math

import numpy as np
import jax
import jax.numpy as jnp
from jax import lax
from jax.experimental import pallas as pl
from jax.experimental.pallas import tpu as pltpu

D_MODEL = 1024
DEPTH = 2
PAST_LEN = 2048
D_FF = 2816
BRANCH_W = 512
RET_HEADS = 4
RET_DK = 64
RET_DV = 128
RET_THETA = 10000.0
RET_CHUNK = 128
HG_HEADS = 4
HG_DK = 128
HG_DV = 128
HG_CHUNK = 64
DIL_GROUPS = ((128, 1), (512, 4), (2048, 16))
N_GROUPS = 3
DIL_HEADS = 8
DIL_DH = 64
ROPE_DIM = DIL_DH // 4
ROPE_THETA = 500000.0
DIL_QB = 128
MEM_LEN = 256
X_HEADS = 4
X_DH = D_MODEL // X_HEADS
IN_COLS = 11264
ALPHA = (2 * DEPTH) ** 0.25
EPS = 1e-5
NEG = -1e30
EXP_CLIP = 80.0
F32 = jnp.float32
BF16 = jnp.bfloat16

COL_AQK = 0
COL_AV = 1
COL_AG = 2
COL_BQ = 3
COL_BF = 4
COL_BI = 5
COL_BG = 6
COL_CQ = 7
COL_CK = 10
COL_CV = 13
COL_GATE = 16
N_COLBLK = IN_COLS // 512

VMEM_LIMIT = 56 * 1024 * 1024


def _cparams(*sem):
    return pltpu.CompilerParams(dimension_semantics=sem, vmem_limit_bytes=VMEM_LIMIT)


def _dot(a, b):
    return jnp.dot(a, b, preferred_element_type=F32)


def _dot_nt(a, b):
    return lax.dot_general(a, b, (((1,), (1,)), ((), ())), preferred_element_type=F32)


def _dot_tn(a, b):
    return lax.dot_general(a, b, (((0,), (0,)), ((), ())), preferred_element_type=F32)


def _ln(y, g, b):
    yc = y - jnp.mean(y, -1, keepdims=True)
    var = jnp.mean(yc * yc, -1, keepdims=True)
    return yc * lax.rsqrt(var + EPS) * g + b


def _silu(x):
    return x * jax.nn.sigmoid(x)


def _ffn_kernel(x_ref, wg_ref, wu_ref, wd_ref, g_ref, b_ref, o_ref, acc_ref):
    j = pl.program_id(1)

    @pl.when(j == 0)
    def _():
        acc_ref[...] = jnp.zeros_like(acc_ref)

    xb = x_ref[...].astype(BF16)
    hg = _dot(xb, wg_ref[...])
    hu = _dot(xb, wu_ref[...])
    hid = (_silu(hg) * hu).astype(BF16)
    acc_ref[...] += _dot(hid, wd_ref[...])

    @pl.when(j == pl.num_programs(1) - 1)
    def _():
        o_ref[...] = _ln(ALPHA * x_ref[...] + 0.5 * acc_ref[...], g_ref[...], b_ref[...])


def _ffn_ln(x, wg, wu, wd, g, b, tm, tf=256):
    n = x.shape[0]
    return pl.pallas_call(
        _ffn_kernel,
        grid=(n // tm, D_FF // tf),
        in_specs=[
            pl.BlockSpec((tm, D_MODEL), lambda i, j: (i, 0)),
            pl.BlockSpec((D_MODEL, tf), lambda i, j: (0, j)),
            pl.BlockSpec((D_MODEL, tf), lambda i, j: (0, j)),
            pl.BlockSpec((tf, D_MODEL), lambda i, j: (j, 0)),
            pl.BlockSpec((1, D_MODEL), lambda i, j: (0, 0)),
            pl.BlockSpec((1, D_MODEL), lambda i, j: (0, 0)),
        ],
        out_specs=pl.BlockSpec((tm, D_MODEL), lambda i, j: (i, 0)),
        out_shape=jax.ShapeDtypeStruct((n, D_MODEL), F32),
        scratch_shapes=[pltpu.VMEM((tm, D_MODEL), F32)],
        compiler_params=_cparams("parallel", "arbitrary"),
        name="ffn_ln",
    )(x, wg, wu, wd, g, b)


def _rot128(y, c, s_lo, s_hi, half):
    return y * c + pltpu.roll(y, 128 - half, 1) * s_lo + pltpu.roll(y, half, 1) * s_hi


def _inproj_kernel(x_ref, w_ref, tab_ref, o_ref):
    j = pl.program_id(1)
    y = _dot(x_ref[...].astype(BF16), w_ref[...])
    is_ret = j == COL_AQK
    is_dil = jnp.logical_and(j >= COL_CQ, j < COL_CV)

    @pl.when(is_ret)
    def _():
        for c in range(4):
            sl = slice(c * 128, (c + 1) * 128)
            o_ref[:, sl] = _rot128(y[:, sl], tab_ref[0], tab_ref[1], tab_ref[2], RET_DK // 2)

    @pl.when(is_dil)
    def _():
        for c in range(4):
            sl = slice(c * 128, (c + 1) * 128)
            o_ref[:, sl] = _rot128(y[:, sl], tab_ref[3], tab_ref[4], tab_ref[5], ROPE_DIM // 2)

    @pl.when(jnp.logical_not(jnp.logical_or(is_ret, is_dil)))
    def _():
        o_ref[...] = y


def _inproj(x, w, tabs, tm):
    n = x.shape[0]
    nt = tabs.shape[1] // tm
    return pl.pallas_call(
        _inproj_kernel,
        grid=(n // tm, N_COLBLK),
        in_specs=[
            pl.BlockSpec((tm, D_MODEL), lambda i, j: (i, 0)),
            pl.BlockSpec((D_MODEL, 512), lambda i, j: (0, j)),
            pl.BlockSpec((6, tm, 128), lambda i, j: (0, i % nt, 0)),
        ],
        out_specs=pl.BlockSpec((tm, 512), lambda i, j: (i, j)),
        out_shape=jax.ShapeDtypeStruct((n, IN_COLS), F32),
        compiler_params=_cparams("parallel", "arbitrary"),
        name="in_proj_rotary",
    )(x, w, tabs)


def _mm_kernel(x_ref, w_ref, o_ref):
    o_ref[...] = _dot(x_ref[...].astype(BF16), w_ref[...])


def _matmul(x, w, tm, tn=512):
    n, k = x.shape
    m = w.shape[1]
    return pl.pallas_call(
        _mm_kernel,
        grid=(n // tm, m // tn),
        in_specs=[pl.BlockSpec((tm, k), lambda i, j: (i, 0)),
                  pl.BlockSpec((k, tn), lambda i, j: (0, j))],
        out_specs=pl.BlockSpec((tm, tn), lambda i, j: (i, j)),
        out_shape=jax.ShapeDtypeStruct((n, m), F32),
        compiler_params=_cparams("parallel", "arbitrary"),
        name="matmul",
    )(x, w)


def _mm_res_ln_kernel(x_ref, w_ref, r_ref, g_ref, b_ref, o_ref):
    y = _dot(x_ref[...].astype(BF16), w_ref[...])
    o_ref[...] = _ln(ALPHA * r_ref[...] + y, g_ref[...], b_ref[...])


def _matmul_res_ln(x, w, res, g, b, tm):
    n = x.shape[0]
    return pl.pallas_call(
        _mm_res_ln_kernel,
        grid=(n // tm,),
        in_specs=[pl.BlockSpec((tm, D_MODEL), lambda i: (i, 0)),
                  pl.BlockSpec((D_MODEL, D_MODEL), lambda i: (0, 0)),
                  pl.BlockSpec((tm, D_MODEL), lambda i: (i, 0)),
                  pl.BlockSpec((1, D_MODEL), lambda i: (0, 0)),
                  pl.BlockSpec((1, D_MODEL), lambda i: (0, 0))],
        out_specs=pl.BlockSpec((tm, D_MODEL), lambda i: (i, 0)),
        out_shape=jax.ShapeDtypeStruct((n, D_MODEL), F32),
        compiler_params=_cparams("parallel"),
        name="proj_res_ln",
    )(x, w, res, g, b)


def _ret_gammas():
    return [1.0 - 2.0 ** (-5.0 - h) for h in range(RET_HEADS)]


def _ret_kernel(qk_ref, v_ref, s0_ref, din_ref, dq_ref, dk_ref, o_ref, so_ref, r_scr, *, nchunk, dc):
    c = RET_CHUNK

    @pl.when(pl.program_id(1) == 0)
    def _():
        r_scr[...] = s0_ref[0]

    def chunk(ci, carry):
        row = pl.multiple_of(ci * c, c)
        qk = qk_ref[0, pl.ds(row, c), :]
        v = v_ref[0, pl.ds(row, c), :]
        for h in range(RET_HEADS):
            q = qk[:, h * RET_DK:(h + 1) * RET_DK] * (RET_DK ** -0.5)
            k = qk[:, 256 + h * RET_DK:256 + (h + 1) * RET_DK]
            vh = v[:, h * RET_DV:(h + 1) * RET_DV].astype(BF16)
            r = r_scr[h]
            s = _dot_nt(q.astype(BF16), k.astype(BF16)) * din_ref[h]
            o = _dot(s.astype(BF16), vh) + _dot((q * dq_ref[h]).astype(BF16), r.astype(BF16))
            r_scr[h] = r * dc[h] + _dot_tn((k * dk_ref[h]).astype(BF16), vh)
            o_ref[0, pl.ds(row, c), h * RET_DV:(h + 1) * RET_DV] = o
        return carry

    lax.fori_loop(0, nchunk, chunk, 0)

    @pl.when(pl.program_id(1) == pl.num_programs(1) - 1)
    def _():
        so_ref[0] = r_scr[...]


def _retention_prompt(proj3, s0, tt=1024):
    b, t, _ = proj3.shape
    c = RET_CHUNK
    gam = np.array(_ret_gammas(), np.float64)
    i = np.arange(c, dtype=np.float64)
    diff = i[:, None] - i[None, :]
    din = np.where(diff >= 0, gam[:, None, None] ** np.maximum(diff, 0.0), 0.0)
    dq = np.broadcast_to((gam[:, None] ** (i + 1.0))[:, :, None], (RET_HEADS, c, RET_DK))
    dk = np.broadcast_to((gam[:, None] ** (c - 1.0 - i))[:, :, None], (RET_HEADS, c, RET_DK))
    dc = tuple(float(g ** c) for g in gam)
    const = lambda shape: pl.BlockSpec(shape, lambda bi, ti: (0,) * len(shape))
    return pl.pallas_call(
        functools.partial(_ret_kernel, nchunk=tt // c, dc=dc),
        grid=(b, t // tt),
        in_specs=[
            pl.BlockSpec((1, tt, 512), lambda bi, ti: (bi, ti, COL_AQK)),
            pl.BlockSpec((1, tt, 512), lambda bi, ti: (bi, ti, COL_AV)),
            pl.BlockSpec((1, RET_HEADS, RET_DK, RET_DV), lambda bi, ti: (bi, 0, 0, 0)),
            const((RET_HEADS, c, c)), const((RET_HEADS, c, RET_DK)), const((RET_HEADS, c, RET_DK)),
        ],
        out_specs=[
            pl.BlockSpec((1, tt, 512), lambda bi, ti: (bi, ti, 0)),
            pl.BlockSpec((1, RET_HEADS, RET_DK, RET_DV), lambda bi, ti: (bi, 0, 0, 0)),
        ],
        out_shape=[jax.ShapeDtypeStruct((b, t, BRANCH_W), F32),
                   jax.ShapeDtypeStruct((b, RET_HEADS, RET_DK, RET_DV), F32)],
        scratch_shapes=[pltpu.VMEM((RET_HEADS, RET_DK, RET_DV), F32)],
        compiler_params=_cparams("parallel", "arbitrary"),
        name="retention_chunked",
    )(proj3, proj3, s0, jnp.asarray(din, F32), jnp.asarray(dq, F32), jnp.asarray(dk, F32))


def _hgrn_log_f(z, lb):
    log_sig = jnp.minimum(z, 0.0) - jnp.log1p(jnp.exp(-jnp.abs(z)))
    return log_sig + jnp.log1p(lb * jnp.exp(jnp.minimum(-z, EXP_CLIP)))


def _cumsum_rows(g):
    n = g.shape[0]
    row = lax.broadcasted_iota(jnp.int32, g.shape, 0)
    sh = 1
    while sh < n:
        g = g + jnp.where(row >= sh, pltpu.roll(g, sh, 0), 0.0)
        sh *= 2
    return g


def _hgrn_kernel(q_ref, f_ref, i_ref, lb_ref, s0_ref, o_ref, so_ref, s_scr, *, nchunk):
    c = HG_CHUNK
    nslab = c // 8

    @pl.when(pl.program_id(2) == 0)
    def _():
        s_scr[...] = s0_ref[0, 0]

    lb = lb_ref[...]
    rowi = lax.broadcasted_iota(jnp.int32, (8, HG_DK), 0)

    def chunk(ci, carry):
        row = pl.multiple_of(ci * c, c)
        qh = _silu(q_ref[0, pl.ds(row, c), :]) * (HG_DK ** -0.5)
        g = _hgrn_log_f(f_ref[0, pl.ds(row, c), :], lb)
        v = i_ref[0, pl.ds(row, c), :]
        b = _cumsum_rows(g)
        ki = 1.0 - jnp.exp(g)
        s_mat = s_scr[...]
        o = _dot((qh * jnp.exp(b)).astype(BF16), s_mat.astype(BF16))
        o_sl = [o[8 * j:8 * j + 8] for j in range(nslab)]
        b_sl = [b[8 * j:8 * j + 8] for j in range(nslab)]
        q_sl = [qh[8 * j:8 * j + 8] for j in range(nslab)]
        for s in range(c):
            js = s // 8
            bs, ks, vs = b[s:s + 1], ki[s:s + 1], v[s:s + 1]
            for j in range(js, nslab):
                d = b_sl[j] - bs
                if j == js:
                    d = jnp.where(rowi >= s % 8, d, NEG)
                a = jnp.sum(q_sl[j] * ks * jnp.exp(d), axis=-1, keepdims=True)
                o_sl[j] = o_sl[j] + a * vs
        o_ref[0, pl.ds(row, c), :] = jnp.concatenate(o_sl, axis=0)
        b_end = b[c - 1:c]
        decay_col = jnp.broadcast_to(jnp.exp(b_end), (HG_DK, HG_DK)).T
        s_scr[...] = s_mat * decay_col + _dot_tn((ki * jnp.exp(b_end - b)).astype(BF16), v.astype(BF16))
        return carry

    lax.fori_loop(0, nchunk, chunk, 0)

    @pl.when(pl.program_id(2) == pl.num_programs(2) - 1)
    def _():
        so_ref[0, 0] = s_scr[...]


def _hgrn_prompt(proj3, lb, s0, tt=1024):
    b, t, _ = proj3.shape
    col = lambda base: (lambda bi, h, ti: (bi, ti, base * 4 + h))
    return pl.pallas_call(
        functools.partial(_hgrn_kernel, nchunk=tt // HG_CHUNK),
        grid=(b, HG_HEADS, t // tt),
        in_specs=[
            pl.BlockSpec((1, tt, 128), col(COL_BQ)),
            pl.BlockSpec((1, tt, 128), col(COL_BF)),
            pl.BlockSpec((1, tt, 128), col(COL_BI)),
            pl.BlockSpec((1, 128), lambda bi, h, ti: (0, h)),
            pl.BlockSpec((1, 1, HG_DK, HG_DV), lambda bi, h, ti: (bi, h, 0, 0)),
        ],
        out_specs=[
            pl.BlockSpec((1, tt, 128), lambda bi, h, ti: (bi, ti, h)),
            pl.BlockSpec((1, 1, HG_DK, HG_DV), lambda bi, h, ti: (bi, h, 0, 0)),
        ],
        out_shape=[jax.ShapeDtypeStruct((b, t, BRANCH_W), F32),
                   jax.ShapeDtypeStruct((b, HG_HEADS, HG_DK, HG_DV), F32)],
        scratch_shapes=[pltpu.VMEM((HG_DK, HG_DV), F32)],
        compiler_params=_cparams("parallel", "parallel", "arbitrary"),
        name="hgrn2_chunked",
    )(proj3, proj3, proj3, lb, s0)


def _dil_kernel(q_ref, kc_ref, kp_ref, vc_ref, vp_ref, o_ref, lse_ref):
    i = pl.program_id(2)
    qb = DIL_QB
    q = q_ref[0]
    kcat = jnp.concatenate([kp_ref[0], kc_ref[0]], axis=0).astype(BF16)
    vcat = jnp.concatenate([vp_ref[0], vc_ref[0]], axis=0).astype(BF16)
    t = lax.broadcasted_iota(jnp.int32, (qb, 2 * qb), 0)
    s = lax.broadcasted_iota(jnp.int32, (qb, 2 * qb), 1)
    valid = (s - qb <= t) & (s >= t) & ((s >= qb) | (i > 0))
    for h in range(DIL_HEADS):
        sl = slice(h * DIL_DH, (h + 1) * DIL_DH)
        sc = _dot_nt(q[:, sl].astype(BF16), kcat[:, sl]) * (DIL_DH ** -0.5)
        sc = jnp.where(valid, sc, NEG)
        m = jnp.max(sc, -1, keepdims=True)
        p = jnp.exp(sc - m)
        l = jnp.sum(p, -1, keepdims=True)
        o_ref[0, :, sl] = _dot((p / l).astype(BF16), vcat[:, sl])
        lse_ref[0, :, sl] = jnp.broadcast_to(m + jnp.log(l), (qb, DIL_DH))


def _dilated_prompt(proj3, g):
    b, t, _ = proj3.shape
    dil = DIL_GROUPS[g][1]
    tr = t // dil
    assert tr % DIL_QB == 0
    pv = proj3.reshape(b, tr, dil * IN_COLS)
    cur = lambda base: (lambda bi, r, i: (bi, i, r * N_COLBLK + base + g))
    prev = lambda base: (lambda bi, r, i: (bi, jnp.maximum(i - 1, 0), r * N_COLBLK + base + g))
    blk = (1, DIL_QB, 512)
    o, lse = pl.pallas_call(
        _dil_kernel,
        grid=(b, dil, tr // DIL_QB),
        in_specs=[pl.BlockSpec(blk, cur(COL_CQ)),
                  pl.BlockSpec(blk, cur(COL_CK)), pl.BlockSpec(blk, prev(COL_CK)),
                  pl.BlockSpec(blk, cur(COL_CV)), pl.BlockSpec(blk, prev(COL_CV))],
        out_specs=[pl.BlockSpec(blk, lambda bi, r, i: (bi, i, r)),
                   pl.BlockSpec(blk, lambda bi, r, i: (bi, i, r))],
        out_shape=[jax.ShapeDtypeStruct((b, tr, dil * 512), F32),
                   jax.ShapeDtypeStruct((b, tr, dil * 512), F32)],
        compiler_params=_cparams("parallel", "parallel", "arbitrary"),
        name="dilated_attn_%d" % g,
    )(pv, pv, pv, pv, pv)
    return o.reshape(b * t, 512), lse.reshape(b * t, 512)


def _head_norm(x, centre):
    parts = []
    for h in range(4):
        xh = x[:, h * 128:(h + 1) * 128]
        if centre:
            xh = xh - jnp.mean(xh, -1, keepdims=True)
        parts.append(xh * lax.rsqrt(jnp.mean(xh * xh, -1, keepdims=True) + EPS))
    return jnp.concatenate(parts, axis=1)


def _merge_kernel(h_ref, oa_ref, ag_ref, ob_ref, bg_ref, oc_ref, g0_ref, g1_ref, g2_ref,
                  gn_ref, hn_ref, wb_ref, wo_ref, lg_ref, lb_ref, o_ref):
    oa = _head_norm(oa_ref[...], True) * gn_ref[...] * _silu(ag_ref[...])
    ob = _head_norm(ob_ref[...], False) * hn_ref[...] * jax.nn.sigmoid(bg_ref[...])
    merged = jax.nn.sigmoid(g0_ref[...]) * _dot(oa.astype(BF16), wb_ref[0])
    merged += jax.nn.sigmoid(g1_ref[...]) * _dot(ob.astype(BF16), wb_ref[1])
    merged += jax.nn.sigmoid(g2_ref[...]) * _dot(oc_ref[...].astype(BF16), wb_ref[2])
    mix = _dot(merged.astype(BF16), wo_ref[...])
    o_ref[...] = _ln(ALPHA * h_ref[...] + mix, lg_ref[...], lb_ref[...])


def _merge(h, proj, oa, ob, oc, gn, hn, wb, wo, lg, lb, tm=256):
    n = h.shape[0]
    row512 = lambda cb: pl.BlockSpec((tm, 512), lambda i: (i, cb))
    row1024 = lambda cb: pl.BlockSpec((tm, 1024), lambda i: (i, cb))
    const = lambda shape: pl.BlockSpec(shape, lambda i: (0,) * len(shape))
    return pl.pallas_call(
        _merge_kernel,
        grid=(n // tm,),
        in_specs=[row1024(0), row512(0), row512(COL_AG), row512(0), row512(COL_BG), row512(0),
                  row1024(COL_GATE // 2), row1024(COL_GATE // 2 + 1), row1024(COL_GATE // 2 + 2),
                  const((1, 512)), const((1, 512)), const((3, 512, D_MODEL)), const((D_MODEL, D_MODEL)),
                  const((1, D_MODEL)), const((1, D_MODEL))],
        out_specs=row1024(0),
        out_shape=jax.ShapeDtypeStruct((n, D_MODEL), F32),
        compiler_params=_cparams("parallel"),
        name="branch_merge",
    )(h, oa, proj, ob, proj, oc, proj, proj, proj, gn, hn, wb, wo, lg, lb)


def _lse_merge_kernel(o0, o1, o2, l0, l1, l2, oc_ref):
    ls = [l0[...], l1[...], l2[...]]
    m = jnp.maximum(jnp.maximum(ls[0], ls[1]), ls[2])
    es = [jnp.exp(x - m) for x in ls]
    den = es[0] + es[1] + es[2]
    oc_ref[...] = (es[0] / den) * o0[...] + (es[1] / den) * o1[...] + (es[2] / den) * o2[...]


def _lse_merge(os_, ls_, tm=1024):
    n = os_[0].shape[0]
    spec = pl.BlockSpec((tm, 512), lambda i: (i, 0))
    return pl.pallas_call(
        _lse_merge_kernel,
        grid=(n // tm,),
        in_specs=[spec] * 6,
        out_specs=spec,
        out_shape=jax.ShapeDtypeStruct((n, 512), F32),
        compiler_params=_cparams("parallel"),
        name="group_merge",
    )(*os_, *ls_)


def _xattn_kernel(q_ref, k_ref, v_ref, o_ref):
    q = q_ref[0]
    for h in range(X_HEADS):
        sl = slice(h * X_DH, (h + 1) * X_DH)
        s = _dot_nt(q[:, sl].astype(BF16), k_ref[0, :, sl].astype(BF16)) * (X_DH ** -0.5)
        m = jnp.max(s, -1, keepdims=True)
        p = jnp.exp(s - m)
        p = p / jnp.sum(p, -1, keepdims=True)
        o_ref[0, :, sl] = _dot(p.astype(BF16), v_ref[0, :, sl].astype(BF16))


def _xattn(q3, mk, mv, layer, tq):
    b, t, _ = q3.shape
    if layer is None:
        mem_spec = pl.BlockSpec((1, MEM_LEN, D_MODEL), lambda bi, ti: (bi, 0, 0))
    else:
        mem_spec = pl.BlockSpec((None, 1, MEM_LEN, D_MODEL), lambda bi, ti: (layer, bi, 0, 0))
    return pl.pallas_call(
        _xattn_kernel,
        grid=(b, t // tq),
        in_specs=[pl.BlockSpec((1, tq, D_MODEL), lambda bi, ti: (bi, ti, 0)), mem_spec, mem_spec],
        out_specs=pl.BlockSpec((1, tq, D_MODEL), lambda bi, ti: (bi, ti, 0)),
        out_shape=jax.ShapeDtypeStruct((b, t, D_MODEL), F32),
        compiler_params=_cparams("parallel", "arbitrary"),
        name="cross_attn",
    )(q3, mk, mv)


def _pick_col(tile, onehot):
    return jnp.sum(jnp.where(onehot, tile, 0.0), axis=1, keepdims=True)


def _pick_row(slab, sub_hit):
    return jnp.sum(jnp.where(sub_hit, slab, 0.0), axis=0, keepdims=True)


def _ret_s_kernel(q_ref, k_ref, v_ref, s_ref, o_ref, so_ref, qt_scr, kt_scr, *, nt, bb):
    j = pl.program_id(0)
    nb = q_ref.shape[1]

    @pl.when(j == 0)
    def _():
        for t in range(nt):
            for c in range(2):
                sl = slice(c * 128, (c + 1) * 128)
                qt_scr[t, sl, :] = (q_ref[t, :, sl] * (RET_DK ** -0.5)).T
                kt_scr[t, sl, :] = k_ref[t, :, sl].T

    lane = lax.broadcasted_iota(jnp.int32, (RET_DK, nb), 1)
    sub = lax.broadcasted_iota(jnp.int32, (8, 128), 0)
    o_ref[...] = jnp.zeros_like(o_ref)
    gam = _ret_gammas()

    def body(bi, carry):
        bg = j * bb + bi
        onehot = lane == bg
        row_g = pl.multiple_of((bg // 8) * 8, 8)
        row_l = pl.multiple_of((bi // 8) * 8, 8)
        sub_hit = sub == bi % 8
        for h in range(RET_HEADS):
            r = s_ref[bi, h]
            for t in range(nt):
                ksl = slice(h * RET_DK, (h + 1) * RET_DK)
                qc = _pick_col(qt_scr[t, ksl, :], onehot)
                kc = _pick_col(kt_scr[t, ksl, :], onehot)
                vsl = slice(h * RET_DV, (h + 1) * RET_DV)
                vrow = _pick_row(v_ref[t, pl.ds(row_g, 8), vsl], sub_hit)
                r = r * gam[h] + kc * vrow
                orow = jnp.sum(r * qc, axis=0, keepdims=True)
                o_ref[t, pl.ds(row_l, 8), vsl] = jnp.where(sub_hit, orow, o_ref[t, pl.ds(row_l, 8), vsl])
            so_ref[bi, h] = r
        return carry

    lax.fori_loop(0, bb, body, 0)


def _retention_sample(q_t, k_t, v_t, state, layer, bb=16):
    nt, nb, _ = q_t.shape
    full = lambda w: pl.BlockSpec((nt, nb, w), lambda j: (0, 0, 0))
    return pl.pallas_call(
        functools.partial(_ret_s_kernel, nt=nt, bb=bb),
        grid=(nb // bb,),
        in_specs=[full(256), full(256), full(512),
                  pl.BlockSpec((None, bb, RET_HEADS, RET_DK, RET_DV), lambda j: (layer, j, 0, 0, 0))],
        out_specs=[pl.BlockSpec((nt, bb, 512), lambda j: (0, j, 0)),
                   pl.BlockSpec((bb, RET_HEADS, RET_DK, RET_DV), lambda j: (j, 0, 0, 0))],
        out_shape=[jax.ShapeDtypeStruct((nt, nb, 512), F32),
                   jax.ShapeDtypeStruct((nb, RET_HEADS, RET_DK, RET_DV), F32)],
        scratch_shapes=[pltpu.VMEM((nt, 256, nb), F32), pltpu.VMEM((nt, 256, nb), F32)],
        compiler_params=_cparams("arbitrary"),
        name="retention_step",
    )(q_t, k_t, v_t, state)


def _hgrn_s_kernel(q_ref, f_ref, i_ref, lb_ref, s_ref, o_ref, so_ref, qt_scr, ft_scr, kt_scr, *, nt, bb):
    j = pl.program_id(0)
    nb = q_ref.shape[1]

    @pl.when(j == 0)
    def _():
        for t in range(nt):
            for h in range(HG_HEADS):
                sl = slice(h * HG_DK, (h + 1) * HG_DK)
                f = jnp.exp(_hgrn_log_f(f_ref[t, :, sl], lb_ref[:, sl]))
                qt_scr[t, sl, :] = (_silu(q_ref[t, :, sl]) * (HG_DK ** -0.5)).T
                ft_scr[t, sl, :] = f.T
                kt_scr[t, sl, :] = (1.0 - f).T

    lane = lax.broadcasted_iota(jnp.int32, (HG_DK, nb), 1)
    sub = lax.broadcasted_iota(jnp.int32, (8, 128), 0)
    o_ref[...] = jnp.zeros_like(o_ref)

    def body(bi, carry):
        bg = j * bb + bi
        onehot = lane == bg
        row_g = pl.multiple_of((bg // 8) * 8, 8)
        row_l = pl.multiple_of((bi // 8) * 8, 8)
        sub_hit = sub == bi % 8
        for h in range(HG_HEADS):
            sl = slice(h * HG_DK, (h + 1) * HG_DK)
            s_mat = s_ref[bi, h]
            for t in range(nt):
                qc = _pick_col(qt_scr[t, sl, :], onehot)
                fc = _pick_col(ft_scr[t, sl, :], onehot)
                kc = _pick_col(kt_scr[t, sl, :], onehot)
                vrow = _pick_row(i_ref[t, pl.ds(row_g, 8), sl], sub_hit)
                s_mat = s_mat * fc + kc * vrow
                orow = jnp.sum(s_mat * qc, axis=0, keepdims=True)
                o_ref[t, pl.ds(row_l, 8), sl] = jnp.where(sub_hit, orow, o_ref[t, pl.ds(row_l, 8), sl])
            so_ref[bi, h] = s_mat
        return carry

    lax.fori_loop(0, bb, body, 0)


def _hgrn_sample(q_t, f_t, i_t, lb, state, layer, bb=16):
    nt, nb, _ = q_t.shape
    full = pl.BlockSpec((nt, nb, 512), lambda j: (0, 0, 0))
    return pl.pallas_call(
        functools.partial(_hgrn_s_kernel, nt=nt, bb=bb),
        grid=(nb // bb,),
        in_specs=[full, full, full, pl.BlockSpec((1, 512), lambda j: (0, 0)),
                  pl.BlockSpec((None, bb, HG_HEADS, HG_DK, HG_DV), lambda j: (layer, j, 0, 0, 0))],
        out_specs=[pl.BlockSpec((nt, bb, 512), lambda j: (0, j, 0)),
                   pl.BlockSpec((bb, HG_HEADS, HG_DK, HG_DV), lambda j: (j, 0, 0, 0))],
        out_shape=[jax.ShapeDtypeStruct((nt, nb, 512), F32),
                   jax.ShapeDtypeStruct((nb, HG_HEADS, HG_DK, HG_DV), F32)],
        scratch_shapes=[pltpu.VMEM((nt, 512, nb), F32)] * 3,
        compiler_params=_cparams("arbitrary"),
        name="hgrn2_step",
    )(q_t, f_t, i_t, lb, state)


def _dil_s_kernel(*refs, nt, bb):
    new_refs, cache_refs, o_ref = refs[:9], refs[9:15], refs[15]
    hm = (lax.broadcasted_iota(jnp.int32, (DIL_HEADS, 512), 1) // DIL_DH
          == lax.broadcasted_iota(jnp.int32, (DIL_HEADS, 512), 0)).astype(F32)
    mrow = lax.broadcasted_iota(jnp.int32, (DIL_HEADS, 128), 1)
    scale = DIL_DH ** -0.5
    for bi in range(bb):
        outs = [[None] * N_GROUPS for _ in range(nt)]
        lses = [[None] * N_GROUPS for _ in range(nt)]
        for g in range(N_GROUPS):
            qg = new_refs[g][bi]
            kn = new_refs[3 + g][bi]
            vn = new_refs[6 + g][bi]
            kc_ref, vc_ref = cache_refs[2 * g], cache_refs[2 * g + 1]
            for i in range(nt):
                csl = slice(0, 512) if g == 0 else slice(i * 512, (i + 1) * 512)
                qbd = qg[i:i + 1] * hm
                sc = _dot_nt(qbd.astype(BF16), kc_ref[bi, :, csl].astype(BF16)) * scale
                if g == 0:
                    sc = jnp.where(mrow >= i, sc, NEG)
                new_js = list(range(i + 1)) if g == 0 else [i]
                sn = [jnp.sum(qbd * kn[jn:jn + 1], axis=-1, keepdims=True) * scale for jn in new_js]
                m = jnp.max(sc, -1, keepdims=True)
                for x in sn:
                    m = jnp.maximum(m, x)
                p = jnp.exp(sc - m)
                pn = [jnp.exp(x - m) for x in sn]
                l = jnp.sum(p, -1, keepdims=True)
                for x in pn:
                    l = l + x
                o = _dot((p / l).astype(BF16), vc_ref[bi, :, csl].astype(BF16))
                for x, jn in zip(pn, new_js):
                    o = o + (x / l) * vn[jn:jn + 1]
                outs[i][g] = jnp.sum(o * hm, axis=0, keepdims=True)
                lses[i][g] = jnp.sum((m + jnp.log(l)) * hm, axis=0, keepdims=True)
        for i in range(nt):
            ls = lses[i]
            m = jnp.maximum(jnp.maximum(ls[0], ls[1]), ls[2])
            es = [jnp.exp(x - m) for x in ls]
            den = es[0] + es[1] + es[2]
            o_ref[bi, i:i + 1, :] = ((es[0] / den) * outs[i][0] + (es[1] / den) * outs[i][1]
                                     + (es[2] / den) * outs[i][2])


def _dilated_sample(proj_s3, caches, layer, bb=2):
    nb, nt, _ = proj_s3.shape
    views = []
    specs = []
    for g, (kbuf, vbuf) in enumerate(caches):
        dil = DIL_GROUPS[g][1]
        w = kbuf.shape[2]
        assert w == DIL_GROUPS[g][0] and w // dil == 128
        width = min(dil, nt) * 512
        for buf in (kbuf, vbuf):
            views.append(buf.reshape(DEPTH, nb, 128, dil * 512))
            specs.append(pl.BlockSpec((None, bb, 128, width), lambda j: (layer, j, 0, 0)))
    new = lambda cb: pl.BlockSpec((bb, nt, 512), lambda j: (j, 0, cb))
    new_specs = [new(base + g) for base in (COL_CQ, COL_CK, COL_CV) for g in range(N_GROUPS)]
    return pl.pallas_call(
        functools.partial(_dil_s_kernel, nt=nt, bb=bb),
        grid=(nb // bb,),
        in_specs=new_specs + specs,
        out_specs=pl.BlockSpec((bb, nt, 512), lambda j: (j, 0, 0)),
        out_shape=jax.ShapeDtypeStruct((nb, nt, 512), F32),
        compiler_params=_cparams("parallel"),
        name="dilated_attn_step",
    )(*([proj_s3] * 9), *views)


def _rotary_tables(pos):
    pos = jnp.asarray(np.asarray(pos), jnp.int32)
    lane = np.arange(128) % 64
    out = []
    for rot_dim, theta in ((RET_DK, RET_THETA), (ROPE_DIM, ROPE_THETA)):
        half = rot_dim // 2
        inv = jnp.power(jnp.float32(theta), -jnp.arange(half, dtype=F32) / half)
        ang = pos.astype(F32)[:, None] * inv[None, :]
        idx = lane % half
        cos = jnp.where(lane < rot_dim, jnp.cos(ang)[:, idx], 1.0)
        sin = jnp.sin(ang)[:, idx]
        out += [cos, jnp.where(lane < half, -sin, 0.0),
                jnp.where((lane >= half) & (lane < rot_dim), sin, 0.0)]
    return jnp.stack(out, 0).astype(F32)


def _hgrn_lower_bounds(raw):
    p = jax.nn.softmax(raw.astype(F32), axis=0)
    return jnp.cumsum(p, axis=0) - p[0:1]


def _row(v):
    return v.reshape(1, -1)


def kernel(x_prompt, x_sample, mem_prompt, state_ret, state_hgrn, cache_win_k0, cache_win_v0, cache_win_k1, cache_win_v1, cache_win_k2, cache_win_v2, cache_mem_k, cache_mem_v, ln_g, ln_b, ffn_w_gate, ffn_w_up, ffn_w_down, w_in, ret_gn_g, hgrn_lb_raw, hgrn_norm_g, w_branch, w_out, xattn_w_q, xattn_w_k, xattn_w_v, xattn_w_o):
    bp, tp, _ = x_prompt.shape
    bs, ts, _ = x_sample.shape
    np_, ns = bp * tp, bs * ts
    lb_all = _hgrn_lower_bounds(hgrn_lb_raw)
    tab_p = _rotary_tables(np.arange(tp))
    tm_s = ns
    tab_s = _rotary_tables(PAST_LEN + (np.arange(tm_s) % ts))
    caches = ((cache_win_k0, cache_win_v0), (cache_win_k1, cache_win_v1), (cache_win_k2, cache_win_v2))
    cmk = cache_mem_k.reshape(DEPTH, bs, MEM_LEN, D_MODEL)
    cmv = cache_mem_v.reshape(DEPTH, bs, MEM_LEN, D_MODEL)
    mem2 = mem_prompt.reshape(bp * MEM_LEN, D_MODEL)

    xp = x_prompt.reshape(np_, D_MODEL)
    xs = x_sample.reshape(ns, D_MODEL)
    tm_p = 1024
    acc = {k: [] for k in ("ret_p", "hg_p", "mk_p", "mv_p", "ret_s", "hg_s")}
    for g in range(N_GROUPS):
        for k in ("wk%d_p", "wv%d_p", "wk%d_s", "wv%d_s"):
            acc[k % g] = []

    for l in range(DEPTH):
        wg = ffn_w_gate[l].astype(BF16)
        wu = ffn_w_up[l].astype(BF16)
        wd = ffn_w_down[l].astype(BF16)
        win = w_in[l].astype(BF16)
        wb = w_branch[l].astype(BF16)
        wo = w_out[l].astype(BF16)
        wq = xattn_w_q[l].astype(BF16)
        wk = xattn_w_k[l].astype(BF16)
        wv = xattn_w_v[l].astype(BF16)
        wxo = xattn_w_o[l].astype(BF16)
        lng = [_row(ln_g[l, i]) for i in range(4)]
        lnb = [_row(ln_b[l, i]) for i in range(4)]
        gn, hn, lb = _row(ret_gn_g[l]), _row(hgrn_norm_g[l]), _row(lb_all[l])

        mk = _matmul(mem2, wk, tm=min(512, bp * MEM_LEN))
        mv = _matmul(mem2, wv, tm=min(512, bp * MEM_LEN))
        h = _ffn_ln(xp, wg[0], wu[0], wd[0], lng[0], lnb[0], tm_p)
        proj = _inproj(h, win, tab_p, tm_p)
        proj3 = proj.reshape(bp, tp, IN_COLS)
        oa, rp = _retention_prompt(proj3, jnp.zeros((bp, RET_HEADS, RET_DK, RET_DV), F32))
        ob, gp = _hgrn_prompt(proj3, lb, jnp.zeros((bp, HG_HEADS, HG_DK, HG_DV), F32))
        dres = [_dilated_prompt(proj3, g) for g in range(N_GROUPS)]
        oc = _lse_merge([d[0] for d in dres], [d[1] for d in dres])
        h = _merge(h, proj, oa.reshape(np_, 512), ob.reshape(np_, 512), oc, gn, hn, wb, wo, lng[1], lnb[1])
        q = _matmul(h, wq, tm=tm_p)
        xo = _xattn(q.reshape(bp, tp, D_MODEL), mk.reshape(bp, MEM_LEN, D_MODEL),
                    mv.reshape(bp, MEM_LEN, D_MODEL), None, tq=512)
        h = _matmul_res_ln(xo.reshape(np_, D_MODEL), wxo, h, lng[2], lnb[2], tm=512)
        xp = _ffn_ln(h, wg[1], wu[1], wd[1], lng[3], lnb[3], tm_p)

        acc["ret_p"].append(rp)
        acc["hg_p"].append(gp)
        acc["mk_p"].append(mk.reshape(bp, MEM_LEN, X_HEADS, X_DH))
        acc["mv_p"].append(mv.reshape(bp, MEM_LEN, X_HEADS, X_DH))
        for g, (window, _) in enumerate(DIL_GROUPS):
            keep = min(window, tp)
            kcol, vcol = (COL_CK + g) * 512, (COL_CV + g) * 512
            acc["wk%d_p" % g].append(proj3[:, tp - keep:, kcol:kcol + 512].reshape(bp, keep, DIL_HEADS, DIL_DH))
            acc["wv%d_p" % g].append(proj3[:, tp - keep:, vcol:vcol + 512].reshape(bp, keep, DIL_HEADS, DIL_DH))

        h = _ffn_ln(xs, wg[0], wu[0], wd[0], lng[0], lnb[0], tm_s)
        proj = _inproj(h, win, tab_s, tm_s)
        proj3 = proj.reshape(bs, ts, IN_COLS)
        tmaj = lambda lo, hi: proj3[:, :, lo:hi].transpose(1, 0, 2)
        oa, rs = _retention_sample(tmaj(0, 256), tmaj(256, 512), tmaj(512, 1024), state_ret, l)
        ob, gs = _hgrn_sample(tmaj(COL_BQ * 512, COL_BF * 512), tmaj(COL_BF * 512, COL_BI * 512),
                              tmaj(COL_BI * 512, COL_BG * 512), lb, state_hgrn, l)
        oa = oa.transpose(1, 0, 2).reshape(ns, 512)
        ob = ob.transpose(1, 0, 2).reshape(ns, 512)
        oc = _dilated_sample(proj3, caches, l).reshape(ns, 512)
        h = _merge(h, proj, oa, ob, oc, gn, hn, wb, wo, lng[1], lnb[1])
        q = _matmul(h, wq, tm=tm_s)
        xo = _xattn(q.reshape(bs, ts, D_MODEL), cmk, cmv, l, tq=ts)
        h = _matmul_res_ln(xo.reshape(ns, D_MODEL), wxo, h, lng[2], lnb[2], tm=tm_s)
        xs = _ffn_ln(h, wg[1], wu[1], wd[1], lng[3], lnb[3], tm_s)

        acc["ret_s"].append(rs)
        acc["hg_s"].append(gs)
        for g in range(N_GROUPS):
            kcol, vcol = (COL_CK + g) * 512, (COL_CV + g) * 512
            acc["wk%d_s" % g].append(proj3[:, :, kcol:kcol + 512].reshape(bs, ts, DIL_HEADS, DIL_DH))
            acc["wv%d_s" % g].append(proj3[:, :, vcol:vcol + 512].reshape(bs, ts, DIL_HEADS, DIL_DH))

    st = {k: jnp.stack(v, axis=0) for k, v in acc.items()}
    return (xp.reshape(bp, tp, D_MODEL), xs.reshape(bs, ts, D_MODEL),
            st["ret_p"], st["hg_p"],
            st["wk0_p"], st["wv0_p"], st["wk1_p"], st["wv1_p"], st["wk2_p"], st["wv2_p"],
            st["mk_p"], st["mv_p"],
            st["ret_s"], st["hg_s"],
            st["wk0_s"], st["wv0_s"], st["wk1_s"], st["wv1_s"], st["wk2_s"], st["wv2_s"])
```

```python
import functools
import math

import numpy as np
import jax
import jax.numpy as jnp
from jax import lax
from jax.experimental import pallas as pl
from jax.experimental.pallas import tpu as pltpu

D_MODEL = 1024
DEPTH = 2
PAST_LEN = 2048
D_FF = 2816
BRANCH_W = 512
RET_HEADS = 4
RET_DK = 64
RET_DV = 128
RET_THETA = 10000.0
RET_CHUNK = 128
HG_HEADS = 4
HG_DK = 128
HG_DV = 128
HG_CHUNK = 64
DIL_GROUPS = ((128, 1), (512, 4), (2048, 16))
N_GROUPS = 3
DIL_HEADS = 8
DIL_DH = 64
ROPE_DIM = DIL_DH // 4
ROPE_THETA = 500000.0
DIL_QB = 128
MEM_LEN = 256
X_HEADS = 4
X_DH = D_MODEL // X_HEADS
IN_COLS = 11264
ALPHA = (2 * DEPTH) ** 0.25
EPS = 1e-5
NEG = -1e30
EXP_CLIP = 80.0
F32 = jnp.float32
BF16 = jnp.bfloat16

COL_AQK = 0
COL_AV = 1
COL_AG = 2
COL_BQ = 3
COL_BF = 4
COL_BI = 5
COL_BG = 6
COL_C0 = 7
COL_GATE = 10
MAIN_COLS = 16 * 512
W_IN_CQ, W_IN_CK, W_IN_CV, W_IN_GATE = 3584, 5120, 6656, 8192

VMEM_LIMIT = 56 * 1024 * 1024


def _cparams(*sem):
    return pltpu.CompilerParams(dimension_semantics=sem, vmem_limit_bytes=VMEM_LIMIT)


def _dot(a, b):
    return jnp.dot(a, b, preferred_element_type=F32)


def _dot_nt(a, b):
    return lax.dot_general(a, b, (((1,), (1,)), ((), ())), preferred_element_type=F32)


def _dot_tn(a, b):
    return lax.dot_general(a, b, (((0,), (0,)), ((), ())), preferred_element_type=F32)


def _ln(y, g, b):
    yc = y - jnp.mean(y, -1, keepdims=True)
    var = jnp.mean(yc * yc, -1, keepdims=True)
    return yc * lax.rsqrt(var + EPS) * g + b


def _silu(x):
    return x * jax.nn.sigmoid(x)


def _ffn_kernel(x_ref, wg_ref, wu_ref, wd_ref, g_ref, b_ref, o_ref, acc_ref):
    j = pl.program_id(1)

    @pl.when(j == 0)
    def _():
        acc_ref[...] = jnp.zeros_like(acc_ref)

    xb = x_ref[...].astype(BF16)
    hg = _dot(xb, wg_ref[...])
    hu = _dot(xb, wu_ref[...])
    hid = (_silu(hg) * hu).astype(BF16)
    acc_ref[...] += _dot(hid, wd_ref[...])

    @pl.when(j == pl.num_programs(1) - 1)
    def _():
        o_ref[...] = _ln(ALPHA * x_ref[...] + 0.5 * acc_ref[...], g_ref[...], b_ref[...])


def _ffn_ln(x, wg, wu, wd, g, b, tm, tf=256):
    n = x.shape[0]
    return pl.pallas_call(
        _ffn_kernel,
        grid=(n // tm, D_FF // tf),
        in_specs=[
            pl.BlockSpec((tm, D_MODEL), lambda i, j: (i, 0)),
            pl.BlockSpec((D_MODEL, tf), lambda i, j: (0, j)),
            pl.BlockSpec((D_MODEL, tf), lambda i, j: (0, j)),
            pl.BlockSpec((tf, D_MODEL), lambda i, j: (j, 0)),
            pl.BlockSpec((1, D_MODEL), lambda i, j: (0, 0)),
            pl.BlockSpec((1, D_MODEL), lambda i, j: (0, 0)),
        ],
        out_specs=pl.BlockSpec((tm, D_MODEL), lambda i, j: (i, 0)),
        out_shape=jax.ShapeDtypeStruct((n, D_MODEL), F32),
        scratch_shapes=[pltpu.VMEM((tm, D_MODEL), F32)],
        compiler_params=_cparams("parallel", "arbitrary"),
        name="ffn_ln",
    )(x, wg, wu, wd, g, b)


def _rot128(y, c, s_lo, s_hi, half):
    return y * c + pltpu.roll(y, 128 - half, 1) * s_lo + pltpu.roll(y, half, 1) * s_hi


def _inproj_kernel(x_ref, w_ref, tab_ref, o_ref, *, ret_blocks, dil_blocks):
    j = pl.program_id(1)
    y = _dot(x_ref[...].astype(BF16), w_ref[...])

    def among(blocks):
        hit = j == blocks[0]
        for c in blocks[1:]:
            hit = jnp.logical_or(hit, j == c)
        return hit

    plain = None
    for blocks, t0, half in ((ret_blocks, 0, RET_DK // 2), (dil_blocks, 3, ROPE_DIM // 2)):
        if not blocks:
            continue
        hit = among(blocks)
        plain = hit if plain is None else jnp.logical_or(plain, hit)

        @pl.when(hit)
        def _(t0=t0, half=half):
            for c in range(4):
                sl = slice(c * 128, (c + 1) * 128)
                o_ref[:, sl] = _rot128(y[:, sl], tab_ref[t0], tab_ref[t0 + 1], tab_ref[t0 + 2], half)

    @pl.when(jnp.logical_not(plain))
    def _():
        o_ref[...] = y


def _inproj(x, w, tabs, tm, ret_blocks, dil_blocks):
    n = x.shape[0]
    cols = w.shape[1]
    nt = tabs.shape[1] // tm
    return pl.pallas_call(
        functools.partial(_inproj_kernel, ret_blocks=ret_blocks, dil_blocks=dil_blocks),
        grid=(n // tm, cols // 512),
        in_specs=[
            pl.BlockSpec((tm, D_MODEL), lambda i, j: (i, 0)),
            pl.BlockSpec((D_MODEL, 512), lambda i, j: (0, j)),
            pl.BlockSpec((6, tm, 128), lambda i, j: (0, i % nt, 0)),
        ],
        out_specs=pl.BlockSpec((tm, 512), lambda i, j: (i, j)),
        out_shape=jax.ShapeDtypeStruct((n, cols), F32),
        compiler_params=_cparams("parallel", "arbitrary"),
        name="in_proj_rotary",
    )(x, w, tabs)


def _mm_kernel(x_ref, w_ref, o_ref):
    o_ref[...] = _dot(x_ref[...].astype(BF16), w_ref[...])


def _matmul(x, w, tm, tn=512):
    n, k = x.shape
    m = w.shape[1]
    return pl.pallas_call(
        _mm_kernel,
        grid=(n // tm, m // tn),
        in_specs=[pl.BlockSpec((tm, k), lambda i, j: (i, 0)),
                  pl.BlockSpec((k, tn), lambda i, j: (0, j))],
        out_specs=pl.BlockSpec((tm, tn), lambda i, j: (i, j)),
        out_shape=jax.ShapeDtypeStruct((n, m), F32),
        compiler_params=_cparams("parallel", "arbitrary"),
        name="matmul",
    )(x, w)


def _mm_res_ln_kernel(x_ref, w_ref, r_ref, g_ref, b_ref, o_ref):
    y = _dot(x_ref[...].astype(BF16), w_ref[...])
    o_ref[...] = _ln(ALPHA * r_ref[...] + y, g_ref[...], b_ref[...])


def _matmul_res_ln(x, w, res, g, b, tm):
    n = x.shape[0]
    return pl.pallas_call(
        _mm_res_ln_kernel,
        grid=(n // tm,),
        in_specs=[pl.BlockSpec((tm, D_MODEL), lambda i: (i, 0)),
                  pl.BlockSpec((D_MODEL, D_MODEL), lambda i: (0, 0)),
                  pl.BlockSpec((tm, D_MODEL), lambda i: (i, 0)),
                  pl.BlockSpec((1, D_MODEL), lambda i: (0, 0)),
                  pl.BlockSpec((1, D_MODEL), lambda i: (0, 0))],
        out_specs=pl.BlockSpec((tm, D_MODEL), lambda i: (i, 0)),
        out_shape=jax.ShapeDtypeStruct((n, D_MODEL), F32),
        compiler_params=_cparams("parallel"),
        name="proj_res_ln",
    )(x, w, res, g, b)


def _ret_gammas():
    return [1.0 - 2.0 ** (-5.0 - h) for h in range(RET_HEADS)]


def _ret_kernel(qk_ref, v_ref, s0_ref, din_ref, dq_ref, dk_ref, o_ref, so_ref, r_scr, *, nchunk, dc):
    c = RET_CHUNK

    @pl.when(pl.program_id(1) == 0)
    def _():
        r_scr[...] = s0_ref[0]

    def chunk(ci, carry):
        row = pl.multiple_of(ci * c, c)
        qk = qk_ref[0, pl.ds(row, c), :]
        v = v_ref[0, pl.ds(row, c), :]
        for h in range(RET_HEADS):
            q = qk[:, h * RET_DK:(h + 1) * RET_DK] * (RET_DK ** -0.5)
            k = qk[:, 256 + h * RET_DK:256 + (h + 1) * RET_DK]
            vh = v[:, h * RET_DV:(h + 1) * RET_DV].astype(BF16)
            r = r_scr[h]
            s = _dot_nt(q.astype(BF16), k.astype(BF16)) * din_ref[h]
            o = _dot(s.astype(BF16), vh) + _dot((q * dq_ref[h]).astype(BF16), r.astype(BF16))
            r_scr[h] = r * dc[h] + _dot_tn((k * dk_ref[h]).astype(BF16), vh)
            o_ref[0, pl.ds(row, c), h * RET_DV:(h + 1) * RET_DV] = o
        return carry

    lax.fori_loop(0, nchunk, chunk, 0)

    @pl.when(pl.program_id(1) == pl.num_programs(1) - 1)
    def _():
        so_ref[0] = r_scr[...]


def _retention_prompt(proj3, s0, tt=1024):
    b, t, _ = proj3.shape
    c = RET_CHUNK
    gam = np.array(_ret_gammas(), np.float64)
    i = np.arange(c, dtype=np.float64)
    diff = i[:, None] - i[None, :]
    din = np.where(diff >= 0, gam[:, None, None] ** np.maximum(diff, 0.0), 0.0)
    dq = np.broadcast_to((gam[:, None] ** (i + 1.0))[:, :, None], (RET_HEADS, c, RET_DK))
    dk = np.broadcast_to((gam[:, None] ** (c - 1.0 - i))[:, :, None], (RET_HEADS, c, RET_DK))
    dc = tuple(float(g ** c) for g in gam)
    const = lambda shape: pl.BlockSpec(shape, lambda bi, ti: (0,) * len(shape))
    return pl.pallas_call(
        functools.partial(_ret_kernel, nchunk=tt // c, dc=dc),
        grid=(b, t // tt),
        in_specs=[
            pl.BlockSpec((1, tt, 512), lambda bi, ti: (bi, ti, COL_AQK)),
            pl.BlockSpec((1, tt, 512), lambda bi, ti: (bi, ti, COL_AV)),
            pl.BlockSpec((1, RET_HEADS, RET_DK, RET_DV), lambda bi, ti: (bi, 0, 0, 0)),
            const((RET_HEADS, c, c)), const((RET_HEADS, c, RET_DK)), const((RET_HEADS, c, RET_DK)),
        ],
        out_specs=[
            pl.BlockSpec((1, tt, 512), lambda bi, ti: (bi, ti, 0)),
            pl.BlockSpec((1, RET_HEADS, RET_DK, RET_DV), lambda bi, ti: (bi, 0, 0, 0)),
        ],
        out_shape=[jax.ShapeDtypeStruct((b, t, BRANCH_W), F32),
                   jax.ShapeDtypeStruct((b, RET_HEADS, RET_DK, RET_DV), F32)],
        scratch_shapes=[pltpu.VMEM((RET_HEADS, RET_DK, RET_DV), F32)],
        compiler_params=_cparams("parallel", "arbitrary"),
        name="retention_chunked",
    )(proj3, proj3, s0, jnp.asarray(din, F32), jnp.asarray(dq, F32), jnp.asarray(dk, F32))


def _hgrn_log_f(z, lb):
    log_sig = jnp.minimum(z, 0.0) - jnp.log1p(jnp.exp(-jnp.abs(z)))
    return log_sig + jnp.log1p(lb * jnp.exp(jnp.minimum(-z, EXP_CLIP)))


def _cumsum_rows(g):
    n = g.shape[0]
    row = lax.broadcasted_iota(jnp.int32, g.shape, 0)
    sh = 1
    while sh < n:
        g = g + jnp.where(row >= sh, pltpu.roll(g, sh, 0), 0.0)
        sh *= 2
    return g


def _hgrn_kernel(q_ref, f_ref, i_ref, lb_ref, s0_ref, o_ref, so_ref, s_scr, *, nchunk):
    c = HG_CHUNK
    nslab = c // 8

    @pl.when(pl.program_id(2) == 0)
    def _():
        s_scr[...] = s0_ref[0, 0]

    lb = lb_ref[...]
    rowi = lax.broadcasted_iota(jnp.int32, (8, HG_DK), 0)

    def chunk(ci, carry):
        row = pl.multiple_of(ci * c, c)
        qh = _silu(q_ref[0, pl.ds(row, c), :]) * (HG_DK ** -0.5)
        g = _hgrn_log_f(f_ref[0, pl.ds(row, c), :], lb)
        v = i_ref[0, pl.ds(row, c), :]
        b = _cumsum_rows(g)
        ki = 1.0 - jnp.exp(g)
        s_mat = s_scr[...]
        o = _dot((qh * jnp.exp(b)).astype(BF16), s_mat.astype(BF16))
        o_sl = [o[8 * j:8 * j + 8] for j in range(nslab)]
        b_sl = [b[8 * j:8 * j + 8] for j in range(nslab)]
        q_sl = [qh[8 * j:8 * j + 8] for j in range(nslab)]
        for s in range(c):
            js = s // 8
            bs, ks, vs = b[s:s + 1], ki[s:s + 1], v[s:s + 1]
            for j in range(js, nslab):
                d = b_sl[j] - bs
                if j == js:
                    d = jnp.where(rowi >= s % 8, d, NEG)
                a = jnp.sum(q_sl[j] * ks * jnp.exp(d), axis=-1, keepdims=True)
                o_sl[j] = o_sl[j] + a * vs
        o_ref[0, pl.ds(row, c), :] = jnp.concatenate(o_sl, axis=0)
        b_end = b[c - 1:c]
        decay_col = jnp.broadcast_to(jnp.exp(b_end), (HG_DK, HG_DK)).T
        s_scr[...] = s_mat * decay_col + _dot_tn((ki * jnp.exp(b_end - b)).astype(BF16), v.astype(BF16))
        return carry

    lax.fori_loop(0, nchunk, chunk, 0)

    @pl.when(pl.program_id(2) == pl.num_programs(2) - 1)
    def _():
        so_ref[0, 0] = s_scr[...]


def _hgrn_prompt(proj3, lb, s0, tt=1024):
    b, t, _ = proj3.shape
    col = lambda base: (lambda bi, h, ti: (bi, ti, base * 4 + h))
    return pl.pallas_call(
        functools.partial(_hgrn_kernel, nchunk=tt // HG_CHUNK),
        grid=(b, HG_HEADS, t // tt),
        in_specs=[
            pl.BlockSpec((1, tt, 128), col(COL_BQ)),
            pl.BlockSpec((1, tt, 128), col(COL_BF)),
            pl.BlockSpec((1, tt, 128), col(COL_BI)),
            pl.BlockSpec((1, 128), lambda bi, h, ti: (0, h)),
            pl.BlockSpec((1, 1, HG_DK, HG_DV), lambda bi, h, ti: (bi, h, 0, 0)),
        ],
        out_specs=[
            pl.BlockSpec((1, tt, 128), lambda bi, h, ti: (bi, ti, h)),
            pl.BlockSpec((1, 1, HG_DK, HG_DV), lambda bi, h, ti: (bi, h, 0, 0)),
        ],
        out_shape=[jax.ShapeDtypeStruct((b, t, BRANCH_W), F32),
                   jax.ShapeDtypeStruct((b, HG_HEADS, HG_DK, HG_DV), F32)],
        scratch_shapes=[pltpu.VMEM((HG_DK, HG_DV), F32)],
        compiler_params=_cparams("parallel", "parallel", "arbitrary"),
        name="hgrn2_chunked",
    )(proj3, proj3, proj3, lb, s0)


def _dil_kernel(q_ref, kc_ref, kp_ref, vc_ref, vp_ref, o_ref, lse_ref):
    i = pl.program_id(2)
    qb = DIL_QB
    q = q_ref[...]
    kcat = jnp.concatenate([kp_ref[...], kc_ref[...]], axis=0).astype(BF16)
    vcat = jnp.concatenate([vp_ref[...], vc_ref[...]], axis=0).astype(BF16)
    t = lax.broadcasted_iota(jnp.int32, (qb, 2 * qb), 0)
    s = lax.broadcasted_iota(jnp.int32, (qb, 2 * qb), 1)
    valid = (s - qb <= t) & (s >= t) & ((s >= qb) | (i > 0))
    for h in range(DIL_HEADS):
        sl = slice(h * DIL_DH, (h + 1) * DIL_DH)
        sc = _dot_nt(q[:, sl].astype(BF16), kcat[:, sl]) * (DIL_DH ** -0.5)
        sc = jnp.where(valid, sc, NEG)
        m = jnp.max(sc, -1, keepdims=True)
        p = jnp.exp(sc - m)
        l = jnp.sum(p, -1, keepdims=True)
        o_ref[:, sl] = _dot((p / l).astype(BF16), vcat[:, sl])
        lse_ref[:, sl] = jnp.broadcast_to(m + jnp.log(l), (qb, DIL_DH))


def _dilated_prompt(qkv4, col0, g):
    b, dil, tr, _ = qkv4.shape
    assert dil == DIL_GROUPS[g][1] and tr % DIL_QB == 0
    cur = lambda c: (lambda bi, r, i: (bi, r, i, c))
    prev = lambda c: (lambda bi, r, i: (bi, r, jnp.maximum(i - 1, 0), c))
    blk = (None, None, DIL_QB, 512)
    oblk = (None, DIL_QB, 512)
    o, lse = pl.pallas_call(
        _dil_kernel,
        grid=(b, dil, tr // DIL_QB),
        in_specs=[pl.BlockSpec(blk, cur(col0)),
                  pl.BlockSpec(blk, cur(col0 + 1)), pl.BlockSpec(blk, prev(col0 + 1)),
                  pl.BlockSpec(blk, cur(col0 + 2)), pl.BlockSpec(blk, prev(col0 + 2))],
        out_specs=[pl.BlockSpec(oblk, lambda bi, r, i: (bi, i, r)),
                   pl.BlockSpec(oblk, lambda bi, r, i: (bi, i, r))],
        out_shape=[jax.ShapeDtypeStruct((b, tr, dil * 512), F32),
                   jax.ShapeDtypeStruct((b, tr, dil * 512), F32)],
        compiler_params=_cparams("parallel", "parallel", "arbitrary"),
        name="dilated_attn_%d" % g,
    )(qkv4, qkv4, qkv4, qkv4, qkv4)
    return o.reshape(b * tr * dil, 512), lse.reshape(b * tr * dil, 512)


def _head_norm(x, centre):
    parts = []
    for h in range(4):
        xh = x[:, h * 128:(h + 1) * 128]
        if centre:
            xh = xh - jnp.mean(xh, -1, keepdims=True)
        parts.append(xh * lax.rsqrt(jnp.mean(xh * xh, -1, keepdims=True) + EPS))
    return jnp.concatenate(parts, axis=1)


def _merge_kernel(h_ref, oa_ref, ag_ref, ob_ref, bg_ref, oc_ref, g0_ref, g1_ref, g2_ref,
                  gn_ref, hn_ref, wb_ref, wo_ref, lg_ref, lb_ref, o_ref):
    oa = _head_norm(oa_ref[...], True) * gn_ref[...] * _silu(ag_ref[...])
    ob = _head_norm(ob_ref[...], False) * hn_ref[...] * jax.nn.sigmoid(bg_ref[...])
    merged = jax.nn.sigmoid(g0_ref[...]) * _dot(oa.astype(BF16), wb_ref[0])
    merged += jax.nn.sigmoid(g1_ref[...]) * _dot(ob.astype(BF16), wb_ref[1])
    merged += jax.nn.sigmoid(g2_ref[...]) * _dot(oc_ref[...].astype(BF16), wb_ref[2])
    mix = _dot(merged.astype(BF16), wo_ref[...])
    o_ref[...] = _ln(ALPHA * h_ref[...] + mix, lg_ref[...], lb_ref[...])


def _merge(h, proj, oa, ob, oc, gn, hn, wb, wo, lg, lb, tm=256):
    n = h.shape[0]
    row512 = lambda cb: pl.BlockSpec((tm, 512), lambda i: (i, cb))
    row1024 = lambda cb: pl.BlockSpec((tm, 1024), lambda i: (i, cb))
    const = lambda shape: pl.BlockSpec(shape, lambda i: (0,) * len(shape))
    return pl.pallas_call(
        _merge_kernel,
        grid=(n // tm,),
        in_specs=[row1024(0), row512(0), row512(COL_AG), row512(0), row512(COL_BG), row512(0),
                  row1024(COL_GATE // 2), row1024(COL_GATE // 2 + 1), row1024(COL_GATE // 2 + 2),
                  const((1, 512)), const((1, 512)), const((3, 512, D_MODEL)), const((D_MODEL, D_MODEL)),
                  const((1, D_MODEL)), const((1, D_MODEL))],
        out_specs=row1024(0),
        out_shape=jax.ShapeDtypeStruct((n, D_MODEL), F32),
        compiler_params=_cparams("parallel"),
        name="branch_merge",
    )(h, oa, proj, ob, proj, oc, proj, proj, proj, gn, hn, wb, wo, lg, lb)


def _lse_merge_kernel(o0, o1, o2, l0, l1, l2, oc_ref):
    ls = [l0[...], l1[...], l2[...]]
    m = jnp.maximum(jnp.maximum(ls[0], ls[1]), ls[2])
    es = [jnp.exp(x - m) for x in ls]
    den = es[0] + es[1] + es[2]
    oc_ref[...] = (es[0] / den) * o0[...] + (es[1] / den) * o1[...] + (es[2] / den) * o2[...]


def _lse_merge(os_, ls_, tm=1024):
    n = os_[0].shape[0]
    spec = pl.BlockSpec((tm, 512), lambda i: (i, 0))
    return pl.pallas_call(
        _lse_merge_kernel,
        grid=(n // tm,),
        in_specs=[spec] * 6,
        out_specs=spec,
        out_shape=jax.ShapeDtypeStruct((n, 512), F32),
        compiler_params=_cparams("parallel"),
        name="group_merge",
    )(*os_, *ls_)


def _xattn_kernel(q_ref, k_ref, v_ref, o_ref):
    q = q_ref[0]
    for h in range(X_HEADS):
        sl = slice(h * X_DH, (h + 1) * X_DH)
        s = _dot_nt(q[:, sl].astype(BF16), k_ref[0, :, sl].astype(BF16)) * (X_DH ** -0.5)
        m = jnp.max(s, -1, keepdims=True)
        p = jnp.exp(s - m)
        p = p / jnp.sum(p, -1, keepdims=True)
        o_ref[0, :, sl] = _dot(p.astype(BF16), v_ref[0, :, sl].astype(BF16))


def _xattn(q3, mk, mv, layer, tq):
    b, t, _ = q3.shape
    if layer is None:
        mem_spec = pl.BlockSpec((1, MEM_LEN, D_MODEL), lambda bi, ti: (bi, 0, 0))
    else:
        mem_spec = pl.BlockSpec((None, 1, MEM_LEN, D_MODEL), lambda bi, ti: (layer, bi, 0, 0))
    return pl.pallas_call(
        _xattn_kernel,
        grid=(b, t // tq),
        in_specs=[pl.BlockSpec((1, tq, D_MODEL), lambda bi, ti: (bi, ti, 0)), mem_spec, mem_spec],
        out_specs=pl.BlockSpec((1, tq, D_MODEL), lambda bi, ti: (bi, ti, 0)),
        out_shape=jax.ShapeDtypeStruct((b, t, D_MODEL), F32),
        compiler_params=_cparams("parallel", "arbitrary"),
        name="cross_attn",
    )(q3, mk, mv)


def _pick_col(tile, onehot):
    return jnp.sum(jnp.where(onehot, tile, 0.0), axis=1, keepdims=True)


def _pick_row(slab, sub_hit):
    return jnp.sum(jnp.where(sub_hit, slab, 0.0), axis=0, keepdims=True)


def _ret_s_kernel(q_ref, k_ref, v_ref, s_ref, o_ref, so_ref, qt_scr, kt_scr, *, nt, bb):
    j = pl.program_id(0)
    nb = q_ref.shape[1]

    @pl.when(j == 0)
    def _():
        for t in range(nt):
            for c in range(2):
                sl = slice(c * 128, (c + 1) * 128)
                qt_scr[t, sl, :] = (q_ref[t, :, sl] * (RET_DK ** -0.5)).T
                kt_scr[t, sl, :] = k_ref[t, :, sl].T

    lane = lax.broadcasted_iota(jnp.int32, (RET_DK, nb), 1)
    sub = lax.broadcasted_iota(jnp.int32, (8, 128), 0)
    o_ref[...] = jnp.zeros_like(o_ref)
    gam = _ret_gammas()

    def body(bi, carry):
        bg = j * bb + bi
        onehot = lane == bg
        row_g = pl.multiple_of((bg // 8) * 8, 8)
        row_l = pl.multiple_of((bi // 8) * 8, 8)
        sub_hit = sub == bi % 8
        for h in range(RET_HEADS):
            r = s_ref[bi, h]
            for t in range(nt):
                ksl = slice(h * RET_DK, (h + 1) * RET_DK)
                qc = _pick_col(qt_scr[t, ksl, :], onehot)
                kc = _pick_col(kt_scr[t, ksl, :], onehot)
                vsl = slice(h * RET_DV, (h + 1) * RET_DV)
                vrow = _pick_row(v_ref[t, pl.ds(row_g, 8), vsl], sub_hit)
                r = r * gam[h] + kc * vrow
                orow = jnp.sum(r * qc, axis=0, keepdims=True)
                o_ref[t, pl.ds(row_l, 8), vsl] = jnp.where(sub_hit, orow, o_ref[t, pl.ds(row_l, 8), vsl])
            so_ref[bi, h] = r
        return carry

    lax.fori_loop(0, bb, body, 0)


def _retention_sample(q_t, k_t, v_t, state, layer, bb=16):
    nt, nb, _ = q_t.shape
    full = lambda w: pl.BlockSpec((nt, nb, w), lambda j: (0, 0, 0))
    return pl.pallas_call(
        functools.partial(_ret_s_kernel, nt=nt, bb=bb),
        grid=(nb // bb,),
        in_specs=[full(256), full(256), full(512),
                  pl.BlockSpec((None, bb, RET_HEADS, RET_DK, RET_DV), lambda j: (layer, j, 0, 0, 0))],
        out_specs=[pl.BlockSpec((nt, bb, 512), lambda j: (0, j, 0)),
                   pl.BlockSpec((bb, RET_HEADS, RET_DK, RET_DV), lambda j: (j, 0, 0, 0))],
        out_shape=[jax.ShapeDtypeStruct((nt, nb, 512), F32),
                   jax.ShapeDtypeStruct((nb, RET_HEADS, RET_DK, RET_DV), F32)],
        scratch_shapes=[pltpu.VMEM((nt, 256, nb), F32), pltpu.VMEM((nt, 256, nb), F32)],
        compiler_params=_cparams("arbitrary"),
        name="retention_step",
    )(q_t, k_t, v_t, state)


def _hgrn_s_kernel(q_ref, f_ref, i_ref, lb_ref, s_ref, o_ref, so_ref, qt_scr, ft_scr, kt_scr, *, nt, bb):
    j = pl.program_id(0)
    nb = q_ref.shape[1]

    @pl.when(j == 0)
    def _():
        for t in range(nt):
            for h in range(HG_HEADS):
                sl = slice(h * HG_DK, (h + 1) * HG_DK)
                f = jnp.exp(_hgrn_log_f(f_ref[t, :, sl], lb_ref[:, sl]))
                qt_scr[t, sl, :] = (_silu(q_ref[t, :, sl]) * (HG_DK ** -0.5)).T
                ft_scr[t, sl, :] = f.T
                kt_scr[t, sl, :] = (1.0 - f).T

    lane = lax.broadcasted_iota(jnp.int32, (HG_DK, nb), 1)
    sub = lax.broadcasted_iota(jnp.int32, (8, 128), 0)
    o_ref[...] = jnp.zeros_like(o_ref)

    def body(bi, carry):
        bg = j * bb + bi
        onehot = lane == bg
        row_g = pl.multiple_of((bg // 8) * 8, 8)
        row_l = pl.multiple_of((bi // 8) * 8, 8)
        sub_hit = sub == bi % 8
        for h in range(HG_HEADS):
            sl = slice(h * HG_DK, (h + 1) * HG_DK)
            s_mat = s_ref[bi, h]
            for t in range(nt):
                qc = _pick_col(qt_scr[t, sl, :], onehot)
                fc = _pick_col(ft_scr[t, sl, :], onehot)
                kc = _pick_col(kt_scr[t, sl, :], onehot)
                vrow = _pick_row(i_ref[t, pl.ds(row_g, 8), sl], sub_hit)
                s_mat = s_mat * fc + kc * vrow
                orow = jnp.sum(s_mat * qc, axis=0, keepdims=True)
                o_ref[t, pl.ds(row_l, 8), sl] = jnp.where(sub_hit, orow, o_ref[t, pl.ds(row_l, 8), sl])
            so_ref[bi, h] = s_mat
        return carry

    lax.fori_loop(0, bb, body, 0)


def _hgrn_sample(q_t, f_t, i_t, lb, state, layer, bb=16):
    nt, nb, _ = q_t.shape
    full = pl.BlockSpec((nt, nb, 512), lambda j: (0, 0, 0))
    return pl.pallas_call(
        functools.partial(_hgrn_s_kernel, nt=nt, bb=bb),
        grid=(nb // bb,),
        in_specs=[full, full, full, pl.BlockSpec((1, 512), lambda j: (0, 0)),
                  pl.BlockSpec((None, bb, HG_HEADS, HG_DK, HG_DV), lambda j: (layer, j, 0, 0, 0))],
        out_specs=[pl.BlockSpec((nt, bb, 512), lambda j: (0, j, 0)),
                   pl.BlockSpec((bb, HG_HEADS, HG_DK, HG_DV), lambda j: (j, 0, 0, 0))],
        out_shape=[jax.ShapeDtypeStruct((nt, nb, 512), F32),
                   jax.ShapeDtypeStruct((nb, HG_HEADS, HG_DK, HG_DV), F32)],
        scratch_shapes=[pltpu.VMEM((nt, 512, nb), F32)] * 3,
        compiler_params=_cparams("arbitrary"),
        name="hgrn2_step",
    )(q_t, f_t, i_t, lb, state)


def _dil_s_kernel(new_ref, k0_ref, v0_ref, k1_ref, v1_ref, k2_ref, v2_ref, o_ref, qpad_scr, opad_scr, *, nt):
    caches = ((k0_ref, v0_ref), (k1_ref, v1_ref), (k2_ref, v2_ref))
    scale = DIL_DH ** -0.5
    lane = lax.broadcasted_iota(jnp.int32, (DIL_DH, 128), 1)
    sub8 = lax.broadcasted_iota(jnp.int32, (DIL_HEADS, 128), 0)
    lane8 = lax.broadcasted_iota(jnp.int32, (DIL_HEADS, 128), 1)
    col_id = lambda g, i, h: (g * nt + i) * DIL_HEADS + h

    qpad_scr[...] = jnp.zeros_like(qpad_scr)
    opad_scr[...] = jnp.zeros_like(opad_scr)
    for g in range(N_GROUPS):
        for i in range(nt):
            r0 = col_id(g, i, 0)
            qpad_scr[r0:r0 + DIL_HEADS, 0:DIL_DH] = new_ref[0, i, g]
    qt = qpad_scr[...].T[0:DIL_DH]
    qcol = lambda g, i, h: qt[:, col_id(g, i, h):col_id(g, i, h) + 1]

    oc = jnp.zeros((DIL_DH, 128), F32)
    m_all = [[None] * nt for _ in range(N_GROUPS)]
    l_all = [[None] * nt for _ in range(N_GROUPS)]
    pn_all = [[None] * nt for _ in range(N_GROUPS)]
    for g in range(N_GROUPS):
        kc_ref, vc_ref = caches[g]
        dil = DIL_GROUPS[g][1]
        ntile = kc_ref.shape[-1] // 128
        q_t = [new_ref[0, i, g] for i in range(nt)]
        kn = [new_ref[0, j, 3 + g] for j in range(nt)]
        new_score = lambda i, j: jnp.sum(q_t[i] * kn[j], axis=-1, keepdims=True) * scale
        if dil == 1:
            for i in range(nt):
                s = jnp.zeros((DIL_HEADS, 128), F32)
                for h in range(DIL_HEADS):
                    row = jnp.sum(kc_ref[0, h] * qcol(g, i, h), axis=0, keepdims=True)
                    s = jnp.where(sub8 == h, row, s)
                s = jnp.where(lane8 >= i, s * scale, NEG)
                sn = [new_score(i, j) for j in range(i + 1)]
                m = jnp.max(s, -1, keepdims=True)
                for x in sn:
                    m = jnp.maximum(m, x)
                p = jnp.exp(s - m)
                pn = [jnp.exp(x - m) for x in sn]
                l = jnp.sum(p, -1, keepdims=True)
                for x in pn:
                    l = l + x
                for h in range(DIL_HEADS):
                    col = jnp.sum(vc_ref[0, h] * p[h:h + 1, :], axis=1, keepdims=True)
                    oc = jnp.where(lane == col_id(g, i, h), col, oc)
                m_all[g][i], l_all[g][i], pn_all[g][i] = m, l, list(zip(pn, range(i + 1)))
        else:
            assert nt <= dil and 128 % dil == 0
            cls8 = [(lane8 & (dil - 1)) == i for i in range(nt)]
            cls = [(lane & (dil - 1)) == i for i in range(nt)]
            s_t = [jnp.zeros((DIL_HEADS, 128), F32) for _ in range(ntile)]
            for h in range(DIL_HEADS):
                qsel = jnp.zeros((DIL_DH, 128), F32)
                for i in range(nt):
                    qsel = jnp.where(cls[i], qcol(g, i, h), qsel)
                for t in range(ntile):
                    row = jnp.sum(kc_ref[0, h, :, t * 128:(t + 1) * 128] * qsel, axis=0, keepdims=True)
                    s_t[t] = jnp.where(sub8 == h, row, s_t[t])
            s_t = [s * scale for s in s_t]
            smax = s_t[0]
            for t in range(1, ntile):
                smax = jnp.maximum(smax, s_t[t])
            m_tile = jnp.zeros((DIL_HEADS, 128), F32)
            owned = cls8[0]
            for i in range(nt):
                sn = new_score(i, i)
                m = jnp.maximum(jnp.max(jnp.where(cls8[i], smax, NEG), -1, keepdims=True), sn)
                m_all[g][i] = m
                pn_all[g][i] = [(jnp.exp(sn - m), i)]
                m_tile = jnp.where(cls8[i], m, m_tile)
                owned = jnp.logical_or(owned, cls8[i])
            p_t = [jnp.exp(jnp.where(owned, s - m_tile, NEG)) for s in s_t]
            psum = p_t[0]
            for t in range(1, ntile):
                psum = psum + p_t[t]
            for i in range(nt):
                l_all[g][i] = jnp.sum(jnp.where(cls8[i], psum, 0.0), -1, keepdims=True) + pn_all[g][i][0][0]
            for h in range(DIL_HEADS):
                acc = jnp.zeros((DIL_DH, 128), F32)
                for t in range(ntile):
                    acc = acc + vc_ref[0, h, :, t * 128:(t + 1) * 128] * p_t[t][h:h + 1, :]
                for i in range(nt):
                    col = jnp.sum(jnp.where(cls[i], acc, 0.0), axis=1, keepdims=True)
                    oc = jnp.where(lane == col_id(g, i, h), col, oc)

    opad_scr[0:DIL_DH, :] = oc
    ot = opad_scr[...].T
    for i in range(nt):
        outs, lses = [], []
        for g in range(N_GROUPS):
            r0 = col_id(g, i, 0)
            o = ot[r0:r0 + DIL_HEADS, 0:DIL_DH]
            for pj, j in pn_all[g][i]:
                o = o + pj * new_ref[0, j, 6 + g]
            outs.append(o / l_all[g][i])
            lses.append(m_all[g][i] + jnp.log(l_all[g][i]))
        m = jnp.maximum(jnp.maximum(lses[0], lses[1]), lses[2])
        es = [jnp.exp(x - m) for x in lses]
        den = es[0] + es[1] + es[2]
        o_ref[0, i] = (es[0] / den) * outs[0] + (es[1] / den) * outs[1] + (es[2] / den) * outs[2]


def _dilated_sample(new9, caches, layer):
    nb, nt = new9.shape[:2]
    views, specs = [], []
    for g, (kbuf, vbuf) in enumerate(caches):
        w = kbuf.shape[2]
        assert w == DIL_GROUPS[g][0] and w // DIL_GROUPS[g][1] == 128
        for buf in (kbuf, vbuf):
            views.append(jnp.transpose(buf, (0, 1, 3, 4, 2)))
            specs.append(pl.BlockSpec((None, 1, DIL_HEADS, DIL_DH, w), lambda j: (layer, j, 0, 0, 0)))
    return pl.pallas_call(
        functools.partial(_dil_s_kernel, nt=nt),
        grid=(nb,),
        in_specs=[pl.BlockSpec((1, nt, 9, DIL_HEADS, DIL_DH), lambda j: (j, 0, 0, 0, 0))] + specs,
        out_specs=pl.BlockSpec((1, nt, DIL_HEADS, DIL_DH), lambda j: (j, 0, 0, 0)),
        out_shape=jax.ShapeDtypeStruct((nb, nt, DIL_HEADS, DIL_DH), F32),
        scratch_shapes=[pltpu.VMEM((128, 128), F32), pltpu.VMEM((128, 128), F32)],
        compiler_params=_cparams("parallel"),
        name="dilated_attn_step",
    )(new9, *views)


def _rotary_tables(pos):
    pos = jnp.asarray(np.asarray(pos), jnp.int32)
    lane = np.arange(128) % 64
    out = []
    for rot_dim, theta in ((RET_DK, RET_THETA), (ROPE_DIM, ROPE_THETA)):
        half = rot_dim // 2
        inv = jnp.power(jnp.float32(theta), -jnp.arange(half, dtype=F32) / half)
        ang = pos.astype(F32)[:, None] * inv[None, :]
        idx = lane % half
        cos = jnp.where(lane < rot_dim, jnp.cos(ang)[:, idx], 1.0)
        sin = jnp.sin(ang)[:, idx]
        out += [cos, jnp.where(lane < half, -sin, 0.0),
                jnp.where((lane >= half) & (lane < rot_dim), sin, 0.0)]
    return jnp.stack(out, 0).astype(F32)


def _hgrn_lower_bounds(raw):
    p = jax.nn.softmax(raw.astype(F32), axis=0)
    return jnp.cumsum(p, axis=0) - p[0:1]


def _row(v):
    return v.reshape(1, -1)


def kernel(x_prompt, x_sample, mem_prompt, state_ret, state_hgrn, cache_win_k0, cache_win_v0, cache_win_k1, cache_win_v1, cache_win_k2, cache_win_v2, cache_mem_k, cache_mem_v, ln_g, ln_b, ffn_w_gate, ffn_w_up, ffn_w_down, w_in, ret_gn_g, hgrn_lb_raw, hgrn_norm_g, w_branch, w_out, xattn_w_q, xattn_w_k, xattn_w_v, xattn_w_o):
    bp, tp, _ = x_prompt.shape
    bs, ts, _ = x_sample.shape
    np_, ns = bp * tp, bs * ts
    lb_all = _hgrn_lower_bounds(hgrn_lb_raw)
    tab_p = _rotary_tables(np.arange(tp))
    tab_res = [tab_p] + [_rotary_tables(np.arange(tp).reshape(tp // dil, dil).T.reshape(-1))
                         for _, dil in DIL_GROUPS[1:]]
    tm_s = ns
    tab_s = _rotary_tables(PAST_LEN + (np.arange(tm_s) % ts))
    caches = ((cache_win_k0, cache_win_v0), (cache_win_k1, cache_win_v1), (cache_win_k2, cache_win_v2))
    cmk = cache_mem_k.reshape(DEPTH, bs, MEM_LEN, D_MODEL)
    cmv = cache_mem_v.reshape(DEPTH, bs, MEM_LEN, D_MODEL)
    mem2 = mem_prompt.reshape(bp * MEM_LEN, D_MODEL)

    xp = x_prompt.reshape(np_, D_MODEL)
    xs = x_sample.reshape(ns, D_MODEL)
    tm_p = 1024
    acc = {k: [] for k in ("ret_p", "hg_p", "mk_p", "mv_p", "ret_s", "hg_s")}
    for g in range(N_GROUPS):
        for k in ("wk%d_p", "wv%d_p", "wk%d_s", "wv%d_s"):
            acc[k % g] = []

    for l in range(DEPTH):
        wg = ffn_w_gate[l].astype(BF16)
        wu = ffn_w_up[l].astype(BF16)
        wd = ffn_w_down[l].astype(BF16)
        win = w_in[l].astype(BF16)
        w_main = jnp.concatenate(
            [win[:, :W_IN_CQ], win[:, W_IN_CQ:W_IN_CQ + 512], win[:, W_IN_CK:W_IN_CK + 512],
             win[:, W_IN_CV:W_IN_CV + 512], win[:, W_IN_GATE:]], axis=1)
        w_grp = [None] + [jnp.concatenate([win[:, o + g * 512:o + (g + 1) * 512]
                                           for o in (W_IN_CQ, W_IN_CK, W_IN_CV)], axis=1)
                          for g in range(1, N_GROUPS)]
        wb = w_branch[l].astype(BF16)
        wo = w_out[l].astype(BF16)
        wq = xattn_w_q[l].astype(BF16)
        wk = xattn_w_k[l].astype(BF16)
        wv = xattn_w_v[l].astype(BF16)
        wxo = xattn_w_o[l].astype(BF16)
        lng = [_row(ln_g[l, i]) for i in range(4)]
        lnb = [_row(ln_b[l, i]) for i in range(4)]
        gn, hn, lb = _row(ret_gn_g[l]), _row(hgrn_norm_g[l]), _row(lb_all[l])

        mk = _matmul(mem2, wk, tm=min(512, bp * MEM_LEN))
        mv = _matmul(mem2, wv, tm=min(512, bp * MEM_LEN))
        h = _ffn_ln(xp, wg[0], wu[0], wd[0], lng[0], lnb[0], tm_p)
        proj = _inproj(h, w_main, tab_p, tm_p, (COL_AQK,), (COL_C0, COL_C0 + 1))
        proj3 = proj.reshape(bp, tp, MAIN_COLS)
        oa, rp = _retention_prompt(proj3, jnp.zeros((bp, RET_HEADS, RET_DK, RET_DV), F32))
        ob, gp = _hgrn_prompt(proj3, lb, jnp.zeros((bp, HG_HEADS, HG_DK, HG_DV), F32))
        dres = [_dilated_prompt(proj.reshape(bp, 1, tp, MAIN_COLS), COL_C0, 0)]
        pgrp = [None]
        for g in range(1, N_GROUPS):
            dil = DIL_GROUPS[g][1]
            h_res = h.reshape(bp, tp // dil, dil, D_MODEL).transpose(0, 2, 1, 3).reshape(np_, D_MODEL)
            pg = _inproj(h_res, w_grp[g], tab_res[g], tm_p, (), (0, 1)).reshape(bp, dil, tp // dil, 1536)
            pgrp.append(pg)
            dres.append(_dilated_prompt(pg, 0, g))
        oc = _lse_merge([d[0] for d in dres], [d[1] for d in dres])
        h = _merge(h, proj, oa.reshape(np_, 512), ob.reshape(np_, 512), oc, gn, hn, wb, wo, lng[1], lnb[1])
        q = _matmul(h, wq, tm=tm_p)
        xo = _xattn(q.reshape(bp, tp, D_MODEL), mk.reshape(bp, MEM_LEN, D_MODEL),
                    mv.reshape(bp, MEM_LEN, D_MODEL), None, tq=512)
        h = _matmul_res_ln(xo.reshape(np_, D_MODEL), wxo, h, lng[2], lnb[2], tm=512)
        xp = _ffn_ln(h, wg[1], wu[1], wd[1], lng[3], lnb[3], tm_p)

        acc["ret_p"].append(rp)
        acc["hg_p"].append(gp)
        acc["mk_p"].append(mk.reshape(bp, MEM_LEN, X_HEADS, X_DH))
        acc["mv_p"].append(mv.reshape(bp, MEM_LEN, X_HEADS, X_DH))
        for g, (window, dil) in enumerate(DIL_GROUPS):
            keep = min(window, tp)
            assert keep % dil == 0
            for name, off in (("wk%d_p", 1), ("wv%d_p", 2)):
                if g == 0:
                    c0 = (COL_C0 + off) * 512
                    rows = proj3[:, tp - keep:, c0:c0 + 512]
                else:
                    rows = pgrp[g][:, :, (tp - keep) // dil:, off * 512:(off + 1) * 512]
                    rows = rows.transpose(0, 2, 1, 3)
                acc[name % g].append(rows.reshape(bp, keep, DIL_HEADS, DIL_DH))

        h = _ffn_ln(xs, wg[0], wu[0], wd[0], lng[0], lnb[0], tm_s)
        proj = _inproj(h, w_main, tab_s, tm_s, (COL_AQK,), (COL_C0, COL_C0 + 1))
        proj3 = proj.reshape(bs, ts, MAIN_COLS)
        tmaj = lambda lo, hi: proj3[:, :, lo:hi].transpose(1, 0, 2)
        oa, rs = _retention_sample(tmaj(0, 256), tmaj(256, 512), tmaj(512, 1024), state_ret, l)
        ob, gs = _hgrn_sample(tmaj(COL_BQ * 512, COL_BF * 512), tmaj(COL_BF * 512, COL_BI * 512),
                              tmaj(COL_BI * 512, COL_BG * 512), lb, state_hgrn, l)
        oa = oa.transpose(1, 0, 2).reshape(ns, 512)
        ob = ob.transpose(1, 0, 2).reshape(ns, 512)
        qkv_s = [proj[:, COL_C0 * 512:(COL_C0 + 3) * 512]]
        qkv_s += [_inproj(h, w_grp[g], tab_s, tm_s, (), (0, 1)) for g in range(1, N_GROUPS)]
        new9 = jnp.stack([qkv_s[g][:, c * 512:(c + 1) * 512].reshape(bs, ts, DIL_HEADS, DIL_DH)
                          for c in range(3) for g in range(N_GROUPS)], axis=2)
        oc = _dilated_sample(new9, caches, l).reshape(ns, 512)
        h = _merge(h, proj, oa, ob, oc, gn, hn, wb, wo, lng[1], lnb[1])
        q = _matmul(h, wq, tm=tm_s)
        xo = _xattn(q.reshape(bs, ts, D_MODEL), cmk, cmv, l, tq=ts)
        h = _matmul_res_ln(xo.reshape(ns, D_MODEL), wxo, h, lng[2], lnb[2], tm=tm_s)
        xs = _ffn_ln(h, wg[1], wu[1], wd[1], lng[3], lnb[3], tm_s)

        acc["ret_s"].append(rs)
        acc["hg_s"].append(gs)
        for g in range(N_GROUPS):
            acc["wk%d_s" % g].append(new9[:, :, 3 + g])
            acc["wv%d_s" % g].append(new9[:, :, 6 + g])

    st = {k: jnp.stack(v, axis=0) for k, v in acc.items()}
    return (xp.reshape(bp, tp, D_MODEL), xs.reshape(bs, ts, D_MODEL),
            st["ret_p"], st["hg_p"],
            st["wk0_p"], st["wv0_p"], st["wk1_p"], st["wv1_p"], st["wk2_p"], st["wv2_p"],
            st["mk_p"], st["mv_p"],
            st["ret_s"], st["hg_s"],
            st["wk0_s"], st["wv0_s"], st["wk1_s"], st["wv1_s"], st["wk2_s"], st["wv2_s"])
```

```python
import functools
import math

import numpy as np
import jax
import jax.numpy as jnp
from jax import lax
from jax.experimental import pallas as pl
from jax.experimental.pallas import tpu as pltpu

D_MODEL = 1024
DEPTH = 2
PAST_LEN = 2048
D_FF = 2816
BRANCH_W = 512
RET_HEADS = 4
RET_DK = 64
RET_DV = 128
RET_THETA = 10000.0
RET_CHUNK = 128
HG_HEADS = 4
HG_DK = 128
HG_DV = 128
HG_CHUNK = 64
DIL_GROUPS = ((128, 1), (512, 4), (2048, 16))
N_GROUPS = 3
DIL_HEADS = 8
DIL_DH = 64
ROPE_DIM = DIL_DH // 4
ROPE_THETA = 500000.0
DIL_QB = 128
MEM_LEN = 256
X_HEADS = 4
X_DH = D_MODEL // X_HEADS
IN_COLS = 11264
ALPHA = (2 * DEPTH) ** 0.25
EPS = 1e-5
NEG = -1e30
EXP_CLIP = 80.0
F32 = jnp.float32
BF16 = jnp.bfloat16

COL_AQK = 0
COL_AV = 1
COL_AG = 2
COL_BQ = 3
COL_BF = 4
COL_BI = 5
COL_BG = 6
COL_C0 = 7
COL_GATE = 10
MAIN_COLS = 16 * 512
W_IN_CQ, W_IN_CK, W_IN_CV, W_IN_GATE = 3584, 5120, 6656, 8192

VMEM_LIMIT = 56 * 1024 * 1024


def _cparams(*sem):
    return pltpu.CompilerParams(dimension_semantics=sem, vmem_limit_bytes=VMEM_LIMIT)


def _dot(a, b):
    return jnp.dot(a, b, preferred_element_type=F32)


def _dot_nt(a, b):
    return lax.dot_general(a, b, (((1,), (1,)), ((), ())), preferred_element_type=F32)


def _dot_tn(a, b):
    return lax.dot_general(a, b, (((0,), (0,)), ((), ())), preferred_element_type=F32)


def _ln(y, g, b):
    yc = y - jnp.mean(y, -1, keepdims=True)
    var = jnp.mean(yc * yc, -1, keepdims=True)
    return yc * lax.rsqrt(var + EPS) * g + b


def _silu(x):
    return x * jax.nn.sigmoid(x)


def _ffn_kernel(x_ref, wg_ref, wu_ref, wd_ref, g_ref, b_ref, o_ref, ob_ref, acc_ref, xb_ref):
    j = pl.program_id(1)

    @pl.when(j == 0)
    def _():
        acc_ref[...] = jnp.zeros_like(acc_ref)
        xb_ref[...] = x_ref[...].astype(BF16)

    xb = xb_ref[...]
    hg = _dot(xb, wg_ref[...])
    hu = _dot(xb, wu_ref[...])
    hid = (_silu(hg) * hu).astype(BF16)
    acc_ref[...] += _dot(hid, wd_ref[...])

    @pl.when(j == pl.num_programs(1) - 1)
    def _():
        y = _ln(ALPHA * x_ref[...] + 0.5 * acc_ref[...], g_ref[...], b_ref[...])
        o_ref[...] = y
        ob_ref[...] = y.astype(BF16)


def _ffn_ln(x, wg, wu, wd, g, b, tm, tf=256):
    n = x.shape[0]
    row = pl.BlockSpec((tm, D_MODEL), lambda i, j: (i, 0))
    return pl.pallas_call(
        _ffn_kernel,
        grid=(n // tm, D_FF // tf),
        in_specs=[
            row,
            pl.BlockSpec((D_MODEL, tf), lambda i, j: (0, j)),
            pl.BlockSpec((D_MODEL, tf), lambda i, j: (0, j)),
            pl.BlockSpec((tf, D_MODEL), lambda i, j: (j, 0)),
            pl.BlockSpec((1, D_MODEL), lambda i, j: (0, 0)),
            pl.BlockSpec((1, D_MODEL), lambda i, j: (0, 0)),
        ],
        out_specs=[row, row],
        out_shape=[jax.ShapeDtypeStruct((n, D_MODEL), F32), jax.ShapeDtypeStruct((n, D_MODEL), BF16)],
        scratch_shapes=[pltpu.VMEM((tm, D_MODEL), F32), pltpu.VMEM((tm, D_MODEL), BF16)],
        compiler_params=_cparams("parallel", "arbitrary"),
        name="ffn_ln",
    )(x, wg, wu, wd, g, b)


def _rot128(y, c, s_lo, s_hi, half):
    return y * c + pltpu.roll(y, 128 - half, 1) * s_lo + pltpu.roll(y, half, 1) * s_hi


def _inproj_kernel(x_ref, w_ref, tab_ref, o_ref, *, ret_blocks, dil_blocks):
    j = pl.program_id(1)
    o_ref[...] = _dot(x_ref[...], w_ref[...])

    def among(blocks):
        hit = j == blocks[0]
        for c in blocks[1:]:
            hit = jnp.logical_or(hit, j == c)
        return hit

    for blocks, t0, half in ((ret_blocks, 0, RET_DK // 2), (dil_blocks, 3, ROPE_DIM // 2)):
        if not blocks:
            continue

        @pl.when(among(blocks))
        def _(t0=t0, half=half):
            for c in range(4):
                sl = slice(c * 128, (c + 1) * 128)
                o_ref[:, sl] = _rot128(o_ref[:, sl], tab_ref[t0], tab_ref[t0 + 1], tab_ref[t0 + 2], half)


def _inproj(x, w, tabs, tm, ret_blocks, dil_blocks):
    n = x.shape[0]
    cols = w.shape[1]
    nt = tabs.shape[1] // tm
    return pl.pallas_call(
        functools.partial(_inproj_kernel, ret_blocks=ret_blocks, dil_blocks=dil_blocks),
        grid=(n // tm, cols // 512),
        in_specs=[
            pl.BlockSpec((tm, D_MODEL), lambda i, j: (i, 0)),
            pl.BlockSpec((D_MODEL, 512), lambda i, j: (0, j)),
            pl.BlockSpec((6, tm, 128), lambda i, j: (0, i % nt, 0)),
        ],
        out_specs=pl.BlockSpec((tm, 512), lambda i, j: (i, j)),
        out_shape=jax.ShapeDtypeStruct((n, cols), F32),
        compiler_params=_cparams("parallel", "arbitrary"),
        name="in_proj_rotary",
    )(x, w, tabs)


def _mm_kernel(x_ref, w_ref, o_ref):
    o_ref[...] = _dot(x_ref[...].astype(BF16), w_ref[...])


def _matmul(x, w, tm, tn=512):
    n, k = x.shape
    m = w.shape[1]
    return pl.pallas_call(
        _mm_kernel,
        grid=(n // tm, m // tn),
        in_specs=[pl.BlockSpec((tm, k), lambda i, j: (i, 0)),
                  pl.BlockSpec((k, tn), lambda i, j: (0, j))],
        out_specs=pl.BlockSpec((tm, tn), lambda i, j: (i, j)),
        out_shape=jax.ShapeDtypeStruct((n, m), F32),
        compiler_params=_cparams("parallel", "arbitrary"),
        name="matmul",
    )(x, w)


def _mm_res_ln_kernel(x_ref, w_ref, r_ref, g_ref, b_ref, o_ref):
    y = _dot(x_ref[...].astype(BF16), w_ref[...])
    o_ref[...] = _ln(ALPHA * r_ref[...] + y, g_ref[...], b_ref[...])


def _matmul_res_ln(x, w, res, g, b, tm):
    n = x.shape[0]
    return pl.pallas_call(
        _mm_res_ln_kernel,
        grid=(n // tm,),
        in_specs=[pl.BlockSpec((tm, D_MODEL), lambda i: (i, 0)),
                  pl.BlockSpec((D_MODEL, D_MODEL), lambda i: (0, 0)),
                  pl.BlockSpec((tm, D_MODEL), lambda i: (i, 0)),
                  pl.BlockSpec((1, D_MODEL), lambda i: (0, 0)),
                  pl.BlockSpec((1, D_MODEL), lambda i: (0, 0))],
        out_specs=pl.BlockSpec((tm, D_MODEL), lambda i: (i, 0)),
        out_shape=jax.ShapeDtypeStruct((n, D_MODEL), F32),
        compiler_params=_cparams("parallel"),
        name="proj_res_ln",
    )(x, w, res, g, b)


def _ret_gammas():
    return [1.0 - 2.0 ** (-5.0 - h) for h in range(RET_HEADS)]


def _ret_kernel(qk_ref, v_ref, s0_ref, din_ref, dq_ref, dk_ref, o_ref, so_ref, r_scr, *, nchunk, dc):
    c = RET_CHUNK

    @pl.when(pl.program_id(1) == 0)
    def _():
        r_scr[...] = s0_ref[0]

    def chunk(ci, carry):
        row = pl.multiple_of(ci * c, c)
        qk = qk_ref[0, pl.ds(row, c), :]
        v = v_ref[0, pl.ds(row, c), :]
        for h in range(RET_HEADS):
            q = qk[:, h * RET_DK:(h + 1) * RET_DK] * (RET_DK ** -0.5)
            k = qk[:, 256 + h * RET_DK:256 + (h + 1) * RET_DK]
            vh = v[:, h * RET_DV:(h + 1) * RET_DV].astype(BF16)
            r = r_scr[h]
            s = _dot_nt(q.astype(BF16), k.astype(BF16)) * din_ref[h]
            o = _dot(s.astype(BF16), vh) + _dot((q * dq_ref[h]).astype(BF16), r.astype(BF16))
            r_scr[h] = r * dc[h] + _dot_tn((k * dk_ref[h]).astype(BF16), vh)
            o_ref[0, pl.ds(row, c), h * RET_DV:(h + 1) * RET_DV] = o
        return carry

    lax.fori_loop(0, nchunk, chunk, 0)

    @pl.when(pl.program_id(1) == pl.num_programs(1) - 1)
    def _():
        so_ref[0] = r_scr[...]


def _retention_prompt(proj3, s0, tt=1024):
    b, t, _ = proj3.shape
    c = RET_CHUNK
    gam = np.array(_ret_gammas(), np.float64)
    i = np.arange(c, dtype=np.float64)
    diff = i[:, None] - i[None, :]
    din = np.where(diff >= 0, gam[:, None, None] ** np.maximum(diff, 0.0), 0.0)
    dq = np.broadcast_to((gam[:, None] ** (i + 1.0))[:, :, None], (RET_HEADS, c, RET_DK))
    dk = np.broadcast_to((gam[:, None] ** (c - 1.0 - i))[:, :, None], (RET_HEADS, c, RET_DK))
    dc = tuple(float(g ** c) for g in gam)
    const = lambda shape: pl.BlockSpec(shape, lambda bi, ti: (0,) * len(shape))
    return pl.pallas_call(
        functools.partial(_ret_kernel, nchunk=tt // c, dc=dc),
        grid=(b, t // tt),
        in_specs=[
            pl.BlockSpec((1, tt, 512), lambda bi, ti: (bi, ti, COL_AQK)),
            pl.BlockSpec((1, tt, 512), lambda bi, ti: (bi, ti, COL_AV)),
            pl.BlockSpec((1, RET_HEADS, RET_DK, RET_DV), lambda bi, ti: (bi, 0, 0, 0)),
            const((RET_HEADS, c, c)), const((RET_HEADS, c, RET_DK)), const((RET_HEADS, c, RET_DK)),
        ],
        out_specs=[
            pl.BlockSpec((1, tt, 512), lambda bi, ti: (bi, ti, 0)),
            pl.BlockSpec((1, RET_HEADS, RET_DK, RET_DV), lambda bi, ti: (bi, 0, 0, 0)),
        ],
        out_shape=[jax.ShapeDtypeStruct((b, t, BRANCH_W), F32),
                   jax.ShapeDtypeStruct((b, RET_HEADS, RET_DK, RET_DV), F32)],
        scratch_shapes=[pltpu.VMEM((RET_HEADS, RET_DK, RET_DV), F32)],
        compiler_params=_cparams("parallel", "arbitrary"),
        name="retention_chunked",
    )(proj3, proj3, s0, jnp.asarray(din, F32), jnp.asarray(dq, F32), jnp.asarray(dk, F32))


def _hgrn_log_f(z, lb):
    log_sig = jnp.minimum(z, 0.0) - jnp.log1p(jnp.exp(-jnp.abs(z)))
    return log_sig + jnp.log1p(lb * jnp.exp(jnp.minimum(-z, EXP_CLIP)))


def _cumsum_rows(g):
    n = g.shape[0]
    row = lax.broadcasted_iota(jnp.int32, g.shape, 0)
    sh = 1
    while sh < n:
        g = g + jnp.where(row >= sh, pltpu.roll(g, sh, 0), 0.0)
        sh *= 2
    return g


def _hgrn_kernel(q_ref, f_ref, i_ref, lb_ref, s0_ref, o_ref, so_ref, s_scr, *, nchunk):
    c = HG_CHUNK
    nslab = c // 8

    @pl.when(pl.program_id(2) == 0)
    def _():
        s_scr[...] = s0_ref[0, 0]

    lb = lb_ref[...]
    sub = 16
    rowi = lax.broadcasted_iota(jnp.int32, (8, HG_DK), 0)
    rowc = lax.broadcasted_iota(jnp.int32, (c, HG_DK), 0)

    def chunk(ci, carry):
        row = pl.multiple_of(ci * c, c)
        qh = _silu(q_ref[0, pl.ds(row, c), :]) * (HG_DK ** -0.5)
        g = _hgrn_log_f(f_ref[0, pl.ds(row, c), :], lb)
        v = i_ref[0, pl.ds(row, c), :]
        b = _cumsum_rows(g)
        ki = 1.0 - jnp.exp(g)
        s_mat = s_scr[...]
        vb = v.astype(BF16)
        o = _dot((qh * jnp.exp(b)).astype(BF16), s_mat.astype(BF16))
        a_blocks = [jnp.zeros((sub, c), F32)]
        for blk in range(1, c // sub):
            lo = blk * sub
            r = b[lo - 1:lo]
            qt = qh[lo:lo + sub] * jnp.exp(b[lo:lo + sub] - r)
            kt = jnp.where(rowc < lo, ki * jnp.exp(jnp.minimum(r - b, 0.0)), 0.0)
            a_blocks.append(_dot_nt(qt.astype(BF16), kt.astype(BF16)))
        o = o + _dot(jnp.concatenate(a_blocks, axis=0).astype(BF16), vb)
        o_sl = [o[8 * j:8 * j + 8] for j in range(nslab)]
        b_sl = [b[8 * j:8 * j + 8] for j in range(nslab)]
        q_sl = [qh[8 * j:8 * j + 8] for j in range(nslab)]
        for s in range(c):
            js = s // 8
            bs, ks, vs = b[s:s + 1], ki[s:s + 1], v[s:s + 1]
            for j in range(js, (s // sub + 1) * (sub // 8)):
                d = b_sl[j] - bs
                if j == js:
                    d = jnp.where(rowi >= s % 8, d, NEG)
                a = jnp.sum(q_sl[j] * ks * jnp.exp(d), axis=-1, keepdims=True)
                o_sl[j] = o_sl[j] + a * vs
        o_ref[0, pl.ds(row, c), :] = jnp.concatenate(o_sl, axis=0)
        b_end = b[c - 1:c]
        decay_col = jnp.broadcast_to(jnp.exp(b_end), (HG_DK, HG_DK)).T
        s_scr[...] = s_mat * decay_col + _dot_tn((ki * jnp.exp(b_end - b)).astype(BF16), v.astype(BF16))
        return carry

    lax.fori_loop(0, nchunk, chunk, 0)

    @pl.when(pl.program_id(2) == pl.num_programs(2) - 1)
    def _():
        so_ref[0, 0] = s_scr[...]


def _hgrn_prompt(proj3, lb, s0, tt=1024):
    b, t, _ = proj3.shape
    col = lambda base: (lambda bi, h, ti: (bi, ti, base * 4 + h))
    return pl.pallas_call(
        functools.partial(_hgrn_kernel, nchunk=tt // HG_CHUNK),
        grid=(b, HG_HEADS, t // tt),
        in_specs=[
            pl.BlockSpec((1, tt, 128), col(COL_BQ)),
            pl.BlockSpec((1, tt, 128), col(COL_BF)),
            pl.BlockSpec((1, tt, 128), col(COL_BI)),
            pl.BlockSpec((1, 128), lambda bi, h, ti: (0, h)),
            pl.BlockSpec((1, 1, HG_DK, HG_DV), lambda bi, h, ti: (bi, h, 0, 0)),
        ],
        out_specs=[
            pl.BlockSpec((1, tt, 128), lambda bi, h, ti: (bi, ti, h)),
            pl.BlockSpec((1, 1, HG_DK, HG_DV), lambda bi, h, ti: (bi, h, 0, 0)),
        ],
        out_shape=[jax.ShapeDtypeStruct((b, t, BRANCH_W), F32),
                   jax.ShapeDtypeStruct((b, HG_HEADS, HG_DK, HG_DV), F32)],
        scratch_shapes=[pltpu.VMEM((HG_DK, HG_DV), F32)],
        compiler_params=_cparams("parallel", "parallel", "arbitrary"),
        name="hgrn2_chunked",
    )(proj3, proj3, proj3, lb, s0)


def _dil_kernel(q_ref, kc_ref, kp_ref, vc_ref, vp_ref, o_ref, lse_ref):
    i = pl.program_id(2)
    qb = DIL_QB
    q = q_ref[...]
    kcat = jnp.concatenate([kp_ref[...], kc_ref[...]], axis=0).astype(BF16)
    vcat = jnp.concatenate([vp_ref[...], vc_ref[...]], axis=0).astype(BF16)
    t = lax.broadcasted_iota(jnp.int32, (qb, 2 * qb), 0)
    s = lax.broadcasted_iota(jnp.int32, (qb, 2 * qb), 1)
    valid = (s - qb <= t) & (s >= t) & ((s >= qb) | (i > 0))
    for h in range(DIL_HEADS):
        sl = slice(h * DIL_DH, (h + 1) * DIL_DH)
        sc = _dot_nt(q[:, sl].astype(BF16), kcat[:, sl]) * (DIL_DH ** -0.5)
        sc = jnp.where(valid, sc, NEG)
        m = jnp.max(sc, -1, keepdims=True)
        p = jnp.exp(sc - m)
        l = jnp.sum(p, -1, keepdims=True)
        o_ref[:, sl] = _dot((p / l).astype(BF16), vcat[:, sl])
        lse_ref[:, sl] = jnp.broadcast_to(m + jnp.log(l), (qb, DIL_DH))


def _dilated_prompt(qkv4, col0, g):
    b, dil, tr, _ = qkv4.shape
    assert dil == DIL_GROUPS[g][1] and tr % DIL_QB == 0
    cur = lambda c: (lambda bi, r, i: (bi, r, i, c))
    prev = lambda c: (lambda bi, r, i: (bi, r, jnp.maximum(i - 1, 0), c))
    blk = (None, None, DIL_QB, 512)
    oblk = (None, DIL_QB, 512)
    o, lse = pl.pallas_call(
        _dil_kernel,
        grid=(b, dil, tr // DIL_QB),
        in_specs=[pl.BlockSpec(blk, cur(col0)),
                  pl.BlockSpec(blk, cur(col0 + 1)), pl.BlockSpec(blk, prev(col0 + 1)),
                  pl.BlockSpec(blk, cur(col0 + 2)), pl.BlockSpec(blk, prev(col0 + 2))],
        out_specs=[pl.BlockSpec(oblk, lambda bi, r, i: (bi, i, r)),
                   pl.BlockSpec(oblk, lambda bi, r, i: (bi, i, r))],
        out_shape=[jax.ShapeDtypeStruct((b, tr, dil * 512), F32),
                   jax.ShapeDtypeStruct((b, tr, dil * 512), F32)],
        compiler_params=_cparams("parallel", "parallel", "arbitrary"),
        name="dilated_attn_%d" % g,
    )(qkv4, qkv4, qkv4, qkv4, qkv4)
    return o.reshape(b * tr * dil, 512), lse.reshape(b * tr * dil, 512)


def _head_norm(x, centre):
    parts = []
    for h in range(4):
        xh = x[:, h * 128:(h + 1) * 128]
        if centre:
            xh = xh - jnp.mean(xh, -1, keepdims=True)
        parts.append(xh * lax.rsqrt(jnp.mean(xh * xh, -1, keepdims=True) + EPS))
    return jnp.concatenate(parts, axis=1)


def _merge_kernel(h_ref, oa_ref, ag_ref, ob_ref, bg_ref, oc_ref, g0_ref, g1_ref, g2_ref,
                  gn_ref, hn_ref, wb_ref, wo_ref, lg_ref, lb_ref, o_ref):
    oa = _head_norm(oa_ref[...], True) * gn_ref[...] * _silu(ag_ref[...])
    ob = _head_norm(ob_ref[...], False) * hn_ref[...] * jax.nn.sigmoid(bg_ref[...])
    merged = jax.nn.sigmoid(g0_ref[...]) * _dot(oa.astype(BF16), wb_ref[0])
    merged += jax.nn.sigmoid(g1_ref[...]) * _dot(ob.astype(BF16), wb_ref[1])
    merged += jax.nn.sigmoid(g2_ref[...]) * _dot(oc_ref[...].astype(BF16), wb_ref[2])
    mix = _dot(merged.astype(BF16), wo_ref[...])
    o_ref[...] = _ln(ALPHA * h_ref[...] + mix, lg_ref[...], lb_ref[...])


def _merge(h, proj, oa, ob, oc, gn, hn, wb, wo, lg, lb, tm=256):
    n = h.shape[0]
    row512 = lambda cb: pl.BlockSpec((tm, 512), lambda i: (i, cb))
    row1024 = lambda cb: pl.BlockSpec((tm, 1024), lambda i: (i, cb))
    const = lambda shape: pl.BlockSpec(shape, lambda i: (0,) * len(shape))
    return pl.pallas_call(
        _merge_kernel,
        grid=(n // tm,),
        in_specs=[row1024(0), row512(0), row512(COL_AG), row512(0), row512(COL_BG), row512(0),
                  row1024(COL_GATE // 2), row1024(COL_GATE // 2 + 1), row1024(COL_GATE // 2 + 2),
                  const((1, 512)), const((1, 512)), const((3, 512, D_MODEL)), const((D_MODEL, D_MODEL)),
                  const((1, D_MODEL)), const((1, D_MODEL))],
        out_specs=row1024(0),
        out_shape=jax.ShapeDtypeStruct((n, D_MODEL), F32),
        compiler_params=_cparams("parallel"),
        name="branch_merge",
    )(h, oa, proj, ob, proj, oc, proj, proj, proj, gn, hn, wb, wo, lg, lb)


def _lse_merge_kernel(o0, o1, o2, l0, l1, l2, oc_ref):
    ls = [l0[...], l1[...], l2[...]]
    m = jnp.maximum(jnp.maximum(ls[0], ls[1]), ls[2])
    es = [jnp.exp(x - m) for x in ls]
    den = es[0] + es[1] + es[2]
    oc_ref[...] = (es[0] / den) * o0[...] + (es[1] / den) * o1[...] + (es[2] / den) * o2[...]


def _lse_merge(os_, ls_, tm=1024):
    n = os_[0].shape[0]
    spec = pl.BlockSpec((tm, 512), lambda i: (i, 0))
    return pl.pallas_call(
        _lse_merge_kernel,
        grid=(n // tm,),
        in_specs=[spec] * 6,
        out_specs=spec,
        out_shape=jax.ShapeDtypeStruct((n, 512), F32),
        compiler_params=_cparams("parallel"),
        name="group_merge",
    )(*os_, *ls_)


def _xattn_kernel(q_ref, k_ref, v_ref, o_ref):
    q = q_ref[0]
    for h in range(X_HEADS):
        sl = slice(h * X_DH, (h + 1) * X_DH)
        s = _dot_nt(q[:, sl].astype(BF16), k_ref[0, :, sl].astype(BF16)) * (X_DH ** -0.5)
        m = jnp.max(s, -1, keepdims=True)
        p = jnp.exp(s - m)
        p = p / jnp.sum(p, -1, keepdims=True)
        o_ref[0, :, sl] = _dot(p.astype(BF16), v_ref[0, :, sl].astype(BF16))


def _xattn_block_kernel(h_ref, wq_ref, k_ref, v_ref, wo_ref, g_ref, b_ref, o_ref):
    h = h_ref[0]
    q = _dot(h.astype(BF16), wq_ref[...])
    outs = []
    for hd in range(X_HEADS):
        sl = slice(hd * X_DH, (hd + 1) * X_DH)
        s = _dot_nt(q[:, sl].astype(BF16), k_ref[0, :, sl]) * (X_DH ** -0.5)
        m = jnp.max(s, -1, keepdims=True)
        p = jnp.exp(s - m)
        p = p / jnp.sum(p, -1, keepdims=True)
        outs.append(_dot(p.astype(BF16), v_ref[0, :, sl]).astype(BF16))
    y = _dot(jnp.concatenate(outs, axis=1), wo_ref[...])
    o_ref[0] = _ln(ALPHA * h + y, g_ref[...], b_ref[...])


def _xattn_block(h3, wq, mk, mv, wo, g, b, tq=512):
    bsz, t, _ = h3.shape
    row = pl.BlockSpec((1, tq, D_MODEL), lambda bi, ti: (bi, ti, 0))
    mem = pl.BlockSpec((1, MEM_LEN, D_MODEL), lambda bi, ti: (bi, 0, 0))
    wspec = pl.BlockSpec((D_MODEL, D_MODEL), lambda bi, ti: (0, 0))
    vec = pl.BlockSpec((1, D_MODEL), lambda bi, ti: (0, 0))
    return pl.pallas_call(
        _xattn_block_kernel,
        grid=(bsz, t // tq),
        in_specs=[row, wspec, mem, mem, wspec, vec, vec],
        out_specs=row,
        out_shape=jax.ShapeDtypeStruct((bsz, t, D_MODEL), F32),
        compiler_params=_cparams("parallel", "arbitrary"),
        name="cross_attn_block",
    )(h3, wq, mk, mv, wo, g, b)


def _xattn(q3, mk, mv, layer, tq):
    b, t, _ = q3.shape
    if layer is None:
        mem_spec = pl.BlockSpec((1, MEM_LEN, D_MODEL), lambda bi, ti: (bi, 0, 0))
    else:
        mem_spec = pl.BlockSpec((None, 1, MEM_LEN, D_MODEL), lambda bi, ti: (layer, bi, 0, 0))
    return pl.pallas_call(
        _xattn_kernel,
        grid=(b, t // tq),
        in_specs=[pl.BlockSpec((1, tq, D_MODEL), lambda bi, ti: (bi, ti, 0)), mem_spec, mem_spec],
        out_specs=pl.BlockSpec((1, tq, D_MODEL), lambda bi, ti: (bi, ti, 0)),
        out_shape=jax.ShapeDtypeStruct((b, t, D_MODEL), F32),
        compiler_params=_cparams("parallel", "arbitrary"),
        name="cross_attn",
    )(q3, mk, mv)


def _pick_col(tile, onehot):
    return jnp.sum(jnp.where(onehot, tile, 0.0), axis=1, keepdims=True)


def _pick_row(slab, sub_hit):
    return jnp.sum(jnp.where(sub_hit, slab, 0.0), axis=0, keepdims=True)


def _ret_s_kernel(q_ref, k_ref, v_ref, s_ref, o_ref, so_ref, qt_scr, kt_scr, *, nt, bb):
    j = pl.program_id(0)
    nb = q_ref.shape[1]

    @pl.when(j == 0)
    def _():
        for t in range(nt):
            for c in range(2):
                sl = slice(c * 128, (c + 1) * 128)
                qt_scr[t, sl, :] = (q_ref[t, :, sl] * (RET_DK ** -0.5)).T
                kt_scr[t, sl, :] = k_ref[t, :, sl].T

    lane = lax.broadcasted_iota(jnp.int32, (RET_DK, nb), 1)
    sub = lax.broadcasted_iota(jnp.int32, (8, 128), 0)
    o_ref[...] = jnp.zeros_like(o_ref)
    gam = _ret_gammas()

    def body(bi, carry):
        bg = j * bb + bi
        onehot = lane == bg
        row_g = pl.multiple_of((bg // 8) * 8, 8)
        row_l = pl.multiple_of((bi // 8) * 8, 8)
        sub_hit = sub == bi % 8
        for h in range(RET_HEADS):
            r = s_ref[bi, h]
            for t in range(nt):
                ksl = slice(h * RET_DK, (h + 1) * RET_DK)
                qc = _pick_col(qt_scr[t, ksl, :], onehot)
                kc = _pick_col(kt_scr[t, ksl, :], onehot)
                vsl = slice(h * RET_DV, (h + 1) * RET_DV)
                vrow = _pick_row(v_ref[t, pl.ds(row_g, 8), vsl], sub_hit)
                r = r * gam[h] + kc * vrow
                orow = jnp.sum(r * qc, axis=0, keepdims=True)
                o_ref[t, pl.ds(row_l, 8), vsl] = jnp.where(sub_hit, orow, o_ref[t, pl.ds(row_l, 8), vsl])
            so_ref[bi, h] = r
        return carry

    lax.fori_loop(0, bb, body, 0)


def _retention_sample(q_t, k_t, v_t, state, layer, bb=16):
    nt, nb, _ = q_t.shape
    full = lambda w: pl.BlockSpec((nt, nb, w), lambda j: (0, 0, 0))
    return pl.pallas_call(
        functools.partial(_ret_s_kernel, nt=nt, bb=bb),
        grid=(nb // bb,),
        in_specs=[full(256), full(256), full(512),
                  pl.BlockSpec((None, bb, RET_HEADS, RET_DK, RET_DV), lambda j: (layer, j, 0, 0, 0))],
        out_specs=[pl.BlockSpec((nt, bb, 512), lambda j: (0, j, 0)),
                   pl.BlockSpec((bb, RET_HEADS, RET_DK, RET_DV), lambda j: (j, 0, 0, 0))],
        out_shape=[jax.ShapeDtypeStruct((nt, nb, 512), F32),
                   jax.ShapeDtypeStruct((nb, RET_HEADS, RET_DK, RET_DV), F32)],
        scratch_shapes=[pltpu.VMEM((nt, 256, nb), F32), pltpu.VMEM((nt, 256, nb), F32)],
        compiler_params=_cparams("arbitrary"),
        name="retention_step",
    )(q_t, k_t, v_t, state)


def _hgrn_s_kernel(q_ref, f_ref, i_ref, lb_ref, s_ref, o_ref, so_ref, qt_scr, ft_scr, kt_scr, *, nt, bb):
    j = pl.program_id(0)
    nb = q_ref.shape[1]

    @pl.when(j == 0)
    def _():
        for t in range(nt):
            for h in range(HG_HEADS):
                sl = slice(h * HG_DK, (h + 1) * HG_DK)
                f = jnp.exp(_hgrn_log_f(f_ref[t, :, sl], lb_ref[:, sl]))
                qt_scr[t, sl, :] = (_silu(q_ref[t, :, sl]) * (HG_DK ** -0.5)).T
                ft_scr[t, sl, :] = f.T
                kt_scr[t, sl, :] = (1.0 - f).T

    lane = lax.broadcasted_iota(jnp.int32, (HG_DK, nb), 1)
    sub = lax.broadcasted_iota(jnp.int32, (8, 128), 0)
    o_ref[...] = jnp.zeros_like(o_ref)

    def body(bi, carry):
        bg = j * bb + bi
        onehot = lane == bg
        row_g = pl.multiple_of((bg // 8) * 8, 8)
        row_l = pl.multiple_of((bi // 8) * 8, 8)
        sub_hit = sub == bi % 8
        for h in range(HG_HEADS):
            sl = slice(h * HG_DK, (h + 1) * HG_DK)
            s_mat = s_ref[bi, h]
            for t in range(nt):
                qc = _pick_col(qt_scr[t, sl, :], onehot)
                fc = _pick_col(ft_scr[t, sl, :], onehot)
                kc = _pick_col(kt_scr[t, sl, :], onehot)
                vrow = _pick_row(i_ref[t, pl.ds(row_g, 8), sl], sub_hit)
                s_mat = s_mat * fc + kc * vrow
                orow = jnp.sum(s_mat * qc, axis=0, keepdims=True)
                o_ref[t, pl.ds(row_l, 8), sl] = jnp.where(sub_hit, orow, o_ref[t, pl.ds(row_l, 8), sl])
            so_ref[bi, h] = s_mat
        return carry

    lax.fori_loop(0, bb, body, 0)


def _hgrn_sample(q_t, f_t, i_t, lb, state, layer, bb=16):
    nt, nb, _ = q_t.shape
    full = pl.BlockSpec((nt, nb, 512), lambda j: (0, 0, 0))
    return pl.pallas_call(
        functools.partial(_hgrn_s_kernel, nt=nt, bb=bb),
        grid=(nb // bb,),
        in_specs=[full, full, full, pl.BlockSpec((1, 512), lambda j: (0, 0)),
                  pl.BlockSpec((None, bb, HG_HEADS, HG_DK, HG_DV), lambda j: (layer, j, 0, 0, 0))],
        out_specs=[pl.BlockSpec((nt, bb, 512), lambda j: (0, j, 0)),
                   pl.BlockSpec((bb, HG_HEADS, HG_DK, HG_DV), lambda j: (j, 0, 0, 0))],
        out_shape=[jax.ShapeDtypeStruct((nt, nb, 512), F32),
                   jax.ShapeDtypeStruct((nb, HG_HEADS, HG_DK, HG_DV), F32)],
        scratch_shapes=[pltpu.VMEM((nt, 512, nb), F32)] * 3,
        compiler_params=_cparams("arbitrary"),
        name="hgrn2_step",
    )(q_t, f_t, i_t, lb, state)


def _dil_s_kernel(new_ref, k0_ref, v0_ref, k1_ref, v1_ref, k2_ref, v2_ref, o_ref, qpad_scr, opad_scr, *, nt):
    caches = ((k0_ref, v0_ref), (k1_ref, v1_ref), (k2_ref, v2_ref))
    scale = DIL_DH ** -0.5
    lane = lax.broadcasted_iota(jnp.int32, (DIL_DH, 128), 1)
    sub8 = lax.broadcasted_iota(jnp.int32, (DIL_HEADS, 128), 0)
    lane8 = lax.broadcasted_iota(jnp.int32, (DIL_HEADS, 128), 1)
    col_id = lambda g, i, h: (g * nt + i) * DIL_HEADS + h

    qpad_scr[...] = jnp.zeros_like(qpad_scr)
    opad_scr[...] = jnp.zeros_like(opad_scr)
    for g in range(N_GROUPS):
        for i in range(nt):
            r0 = col_id(g, i, 0)
            qpad_scr[r0:r0 + DIL_HEADS, 0:DIL_DH] = new_ref[0, i, g]
    qt = qpad_scr[...].T[0:DIL_DH]
    qcol = lambda g, i, h: qt[:, col_id(g, i, h):col_id(g, i, h) + 1]

    oc = jnp.zeros((DIL_DH, 128), F32)
    m_all = [[None] * nt for _ in range(N_GROUPS)]
    l_all = [[None] * nt for _ in range(N_GROUPS)]
    pn_all = [[None] * nt for _ in range(N_GROUPS)]
    for g in range(N_GROUPS):
        kc_ref, vc_ref = caches[g]
        dil = DIL_GROUPS[g][1]
        ntile = kc_ref.shape[-1] // 128
        q_t = [new_ref[0, i, g] for i in range(nt)]
        kn = [new_ref[0, j, 3 + g] for j in range(nt)]
        new_score = lambda i, j: jnp.sum(q_t[i] * kn[j], axis=-1, keepdims=True) * scale
        if dil == 1:
            for i in range(nt):
                s = jnp.zeros((DIL_HEADS, 128), F32)
                for h in range(DIL_HEADS):
                    row = jnp.sum(kc_ref[0, h] * qcol(g, i, h), axis=0, keepdims=True)
                    s = jnp.where(sub8 == h, row, s)
                s = jnp.where(lane8 >= i, s * scale, NEG)
                sn = [new_score(i, j) for j in range(i + 1)]
                m = jnp.max(s, -1, keepdims=True)
                for x in sn:
                    m = jnp.maximum(m, x)
                p = jnp.exp(s - m)
                pn = [jnp.exp(x - m) for x in sn]
                l = jnp.sum(p, -1, keepdims=True)
                for x in pn:
                    l = l + x
                for h in range(DIL_HEADS):
                    col = jnp.sum(vc_ref[0, h] * p[h:h + 1, :], axis=1, keepdims=True)
                    oc = jnp.where(lane == col_id(g, i, h), col, oc)
                m_all[g][i], l_all[g][i], pn_all[g][i] = m, l, list(zip(pn, range(i + 1)))
        else:
            assert nt <= dil and 128 % dil == 0
            cls8 = [(lane8 & (dil - 1)) == i for i in range(nt)]
            cls = [(lane & (dil - 1)) == i for i in range(nt)]
            s_t = [jnp.zeros((DIL_HEADS, 128), F32) for _ in range(ntile)]
            for h in range(DIL_HEADS):
                qsel = jnp.zeros((DIL_DH, 128), F32)
                for i in range(nt):
                    qsel = jnp.where(cls[i], qcol(g, i, h), qsel)
                for t in range(ntile):
                    row = jnp.sum(kc_ref[0, h, :, t * 128:(t + 1) * 128] * qsel, axis=0, keepdims=True)
                    s_t[t] = jnp.where(sub8 == h, row, s_t[t])
            s_t = [s * scale for s in s_t]
            smax = s_t[0]
            for t in range(1, ntile):
                smax = jnp.maximum(smax, s_t[t])
            m_tile = jnp.zeros((DIL_HEADS, 128), F32)
            owned = cls8[0]
            for i in range(nt):
                sn = new_score(i, i)
                m = jnp.maximum(jnp.max(jnp.where(cls8[i], smax, NEG), -1, keepdims=True), sn)
                m_all[g][i] = m
                pn_all[g][i] = [(jnp.exp(sn - m), i)]
                m_tile = jnp.where(cls8[i], m, m_tile)
                owned = jnp.logical_or(owned, cls8[i])
            p_t = [jnp.exp(jnp.where(owned, s - m_tile, NEG)) for s in s_t]
            psum = p_t[0]
            for t in range(1, ntile):
                psum = psum + p_t[t]
            for i in range(nt):
                l_all[g][i] = jnp.sum(jnp.where(cls8[i], psum, 0.0), -1, keepdims=True) + pn_all[g][i][0][0]
            for h in range(DIL_HEADS):
                acc = jnp.zeros((DIL_DH, 128), F32)
                for t in range(ntile):
                    acc = acc + vc_ref[0, h, :, t * 128:(t + 1) * 128] * p_t[t][h:h + 1, :]
                for i in range(nt):
                    col = jnp.sum(jnp.where(cls[i], acc, 0.0), axis=1, keepdims=True)
                    oc = jnp.where(lane == col_id(g, i, h), col, oc)

    opad_scr[0:DIL_DH, :] = oc
    ot = opad_scr[...].T
    for i in range(nt):
        outs, lses = [], []
        for g in range(N_GROUPS):
            r0 = col_id(g, i, 0)
            o = ot[r0:r0 + DIL_HEADS, 0:DIL_DH]
            for pj, j in pn_all[g][i]:
                o = o + pj * new_ref[0, j, 6 + g]
            outs.append(o / l_all[g][i])
            lses.append(m_all[g][i] + jnp.log(l_all[g][i]))
        m = jnp.maximum(jnp.maximum(lses[0], lses[1]), lses[2])
        es = [jnp.exp(x - m) for x in lses]
        den = es[0] + es[1] + es[2]
        o_ref[0, i] = (es[0] / den) * outs[0] + (es[1] / den) * outs[1] + (es[2] / den) * outs[2]


def _dilated_sample(new9, caches, layer):
    nb, nt = new9.shape[:2]
    views, specs = [], []
    for g, (kbuf, vbuf) in enumerate(caches):
        w = kbuf.shape[2]
        assert w == DIL_GROUPS[g][0] and w // DIL_GROUPS[g][1] == 128
        for buf in (kbuf, vbuf):
            views.append(jnp.transpose(buf, (0, 1, 3, 4, 2)))
            specs.append(pl.BlockSpec((None, 1, DIL_HEADS, DIL_DH, w), lambda j: (layer, j, 0, 0, 0)))
    return pl.pallas_call(
        functools.partial(_dil_s_kernel, nt=nt),
        grid=(nb,),
        in_specs=[pl.BlockSpec((1, nt, 9, DIL_HEADS, DIL_DH), lambda j: (j, 0, 0, 0, 0))] + specs,
        out_specs=pl.BlockSpec((1, nt, DIL_HEADS, DIL_DH), lambda j: (j, 0, 0, 0)),
        out_shape=jax.ShapeDtypeStruct((nb, nt, DIL_HEADS, DIL_DH), F32),
        scratch_shapes=[pltpu.VMEM((128, 128), F32), pltpu.VMEM((128, 128), F32)],
        compiler_params=_cparams("parallel"),
        name="dilated_attn_step",
    )(new9, *views)


def _rotary_tables(pos):
    pos = jnp.asarray(np.asarray(pos), jnp.int32)
    lane = np.arange(128) % 64
    out = []
    for rot_dim, theta in ((RET_DK, RET_THETA), (ROPE_DIM, ROPE_THETA)):
        half = rot_dim // 2
        inv = jnp.power(jnp.float32(theta), -jnp.arange(half, dtype=F32) / half)
        ang = pos.astype(F32)[:, None] * inv[None, :]
        idx = lane % half
        cos = jnp.where(lane < rot_dim, jnp.cos(ang)[:, idx], 1.0)
        sin = jnp.sin(ang)[:, idx]
        out += [cos, jnp.where(lane < half, -sin, 0.0),
                jnp.where((lane >= half) & (lane < rot_dim), sin, 0.0)]
    return jnp.stack(out, 0).astype(F32)


def _hgrn_lower_bounds(raw):
    p = jax.nn.softmax(raw.astype(F32), axis=0)
    return jnp.cumsum(p, axis=0) - p[0:1]


def _row(v):
    return v.reshape(1, -1)


def kernel(x_prompt, x_sample, mem_prompt, state_ret, state_hgrn, cache_win_k0, cache_win_v0, cache_win_k1, cache_win_v1, cache_win_k2, cache_win_v2, cache_mem_k, cache_mem_v, ln_g, ln_b, ffn_w_gate, ffn_w_up, ffn_w_down, w_in, ret_gn_g, hgrn_lb_raw, hgrn_norm_g, w_branch, w_out, xattn_w_q, xattn_w_k, xattn_w_v, xattn_w_o):
    bp, tp, _ = x_prompt.shape
    bs, ts, _ = x_sample.shape
    np_, ns = bp * tp, bs * ts
    lb_all = _hgrn_lower_bounds(hgrn_lb_raw)
    tab_p = _rotary_tables(np.arange(tp))
    tab_res = [tab_p] + [_rotary_tables(np.arange(tp).reshape(tp // dil, dil).T.reshape(-1))
                         for _, dil in DIL_GROUPS[1:]]
    tm_s = ns
    tab_s = _rotary_tables(PAST_LEN + (np.arange(tm_s) % ts))
    caches = ((cache_win_k0, cache_win_v0), (cache_win_k1, cache_win_v1), (cache_win_k2, cache_win_v2))
    cmk = cache_mem_k.reshape(DEPTH, bs, MEM_LEN, D_MODEL)
    cmv = cache_mem_v.reshape(DEPTH, bs, MEM_LEN, D_MODEL)
    mem2 = mem_prompt.reshape(bp * MEM_LEN, D_MODEL)

    xp = x_prompt.reshape(np_, D_MODEL)
    xs = x_sample.reshape(ns, D_MODEL)
    tm_p = 1024
    acc = {k: [] for k in ("ret_p", "hg_p", "mk_p", "mv_p", "ret_s", "hg_s")}
    for g in range(N_GROUPS):
        for k in ("wk%d_p", "wv%d_p", "wk%d_s", "wv%d_s"):
            acc[k % g] = []

    for l in range(DEPTH):
        wg = ffn_w_gate[l].astype(BF16)
        wu = ffn_w_up[l].astype(BF16)
        wd = ffn_w_down[l].astype(BF16)
        win = w_in[l].astype(BF16)
        w_main = jnp.concatenate(
            [win[:, :W_IN_CQ], win[:, W_IN_CQ:W_IN_CQ + 512], win[:, W_IN_CK:W_IN_CK + 512],
             win[:, W_IN_CV:W_IN_CV + 512], win[:, W_IN_GATE:]], axis=1)
        w_grp = [None] + [jnp.concatenate([win[:, o + g * 512:o + (g + 1) * 512]
                                           for o in (W_IN_CQ, W_IN_CK, W_IN_CV)], axis=1)
                          for g in range(1, N_GROUPS)]
        wb = w_branch[l].astype(BF16)
        wo = w_out[l].astype(BF16)
        wq = xattn_w_q[l].astype(BF16)
        wk = xattn_w_k[l].astype(BF16)
        wv = xattn_w_v[l].astype(BF16)
        wxo = xattn_w_o[l].astype(BF16)
        lng = [_row(ln_g[l, i]) for i in range(4)]
        lnb = [_row(ln_b[l, i]) for i in range(4)]
        gn, hn, lb = _row(ret_gn_g[l]), _row(hgrn_norm_g[l]), _row(lb_all[l])

        mk = _matmul(mem2, wk, tm=min(512, bp * MEM_LEN))
        mv = _matmul(mem2, wv, tm=min(512, bp * MEM_LEN))
        h, hb = _ffn_ln(xp, wg[0], wu[0], wd[0], lng[0], lnb[0], tm_p)
        proj = _inproj(hb, w_main, tab_p, tm_p, (COL_AQK,), (COL_C0, COL_C0 + 1))
        proj3 = proj.reshape(bp, tp, MAIN_COLS)
        oa, rp = _retention_prompt(proj3, jnp.zeros((bp, RET_HEADS, RET_DK, RET_DV), F32))
        ob, gp = _hgrn_prompt(proj3, lb, jnp.zeros((bp, HG_HEADS, HG_DK, HG_DV), F32))
        dres = [_dilated_prompt(proj.reshape(bp, 1, tp, MAIN_COLS), COL_C0, 0)]
        pgrp = [None]
        for g in range(1, N_GROUPS):
            dil = DIL_GROUPS[g][1]
            h_res = hb.reshape(bp, tp // dil, dil, D_MODEL).transpose(0, 2, 1, 3).reshape(np_, D_MODEL)
            pg = _inproj(h_res, w_grp[g], tab_res[g], tm_p, (), (0, 1)).reshape(bp, dil, tp // dil, 1536)
            pgrp.append(pg)
            dres.append(_dilated_prompt(pg, 0, g))
        oc = _lse_merge([d[0] for d in dres], [d[1] for d in dres])
        h = _merge(h, proj, oa.reshape(np_, 512), ob.reshape(np_, 512), oc, gn, hn, wb, wo, lng[1], lnb[1])
        h = _xattn_block(h.reshape(bp, tp, D_MODEL), wq, mk.astype(BF16).reshape(bp, MEM_LEN, D_MODEL),
                         mv.astype(BF16).reshape(bp, MEM_LEN, D_MODEL), wxo, lng[2], lnb[2])
        xp, _ = _ffn_ln(h.reshape(np_, D_MODEL), wg[1], wu[1], wd[1], lng[3], lnb[3], tm_p)

        acc["ret_p"].append(rp)
        acc["hg_p"].append(gp)
        acc["mk_p"].append(mk.reshape(bp, MEM_LEN, X_HEADS, X_DH))
        acc["mv_p"].append(mv.reshape(bp, MEM_LEN, X_HEADS, X_DH))
        for g, (window, dil) in enumerate(DIL_GROUPS):
            keep = min(window, tp)
            assert keep % dil == 0
            for name, off in (("wk%d_p", 1), ("wv%d_p", 2)):
                if g == 0:
                    c0 = (COL_C0 + off) * 512
                    rows = proj3[:, tp - keep:, c0:c0 + 512]
                else:
                    rows = pgrp[g][:, :, (tp - keep) // dil:, off * 512:(off + 1) * 512]
                    rows = rows.transpose(0, 2, 1, 3)
                acc[name % g].append(rows.reshape(bp, keep, DIL_HEADS, DIL_DH))

        h, hb = _ffn_ln(xs, wg[0], wu[0], wd[0], lng[0], lnb[0], tm_s)
        proj = _inproj(hb, w_main, tab_s, tm_s, (COL_AQK,), (COL_C0, COL_C0 + 1))
        proj3 = proj.reshape(bs, ts, MAIN_COLS)
        tmaj = lambda lo, hi: proj3[:, :, lo:hi].transpose(1, 0, 2)
        oa, rs = _retention_sample(tmaj(0, 256), tmaj(256, 512), tmaj(512, 1024), state_ret, l)
        ob, gs = _hgrn_sample(tmaj(COL_BQ * 512, COL_BF * 512), tmaj(COL_BF * 512, COL_BI * 512),
                              tmaj(COL_BI * 512, COL_BG * 512), lb, state_hgrn, l)
        oa = oa.transpose(1, 0, 2).reshape(ns, 512)
        ob = ob.transpose(1, 0, 2).reshape(ns, 512)
        qkv_s = [proj[:, COL_C0 * 512:(COL_C0 + 3) * 512]]
        qkv_s += [_inproj(hb, w_grp[g], tab_s, tm_s, (), (0, 1)) for g in range(1, N_GROUPS)]
        new9 = jnp.stack([qkv_s[g][:, c * 512:(c + 1) * 512].reshape(bs, ts, DIL_HEADS, DIL_DH)
                          for c in range(3) for g in range(N_GROUPS)], axis=2)
        oc = _dilated_sample(new9, caches, l).reshape(ns, 512)
        h = _merge(h, proj, oa, ob, oc, gn, hn, wb, wo, lng[1], lnb[1])
        q = _matmul(h, wq, tm=tm_s)
        xo = _xattn(q.reshape(bs, ts, D_MODEL), cmk, cmv, l, tq=ts)
        h = _matmul_res_ln(xo.reshape(ns, D_MODEL), wxo, h, lng[2], lnb[2], tm=tm_s)
        xs, _ = _ffn_ln(h, wg[1], wu[1], wd[1], lng[3], lnb[3], tm_s)

        acc["ret_s"].append(rs)
        acc["hg_s"].append(gs)
        for g in range(N_GROUPS):
            acc["wk%d_s" % g].append(new9[:, :, 3 + g])
            acc["wv%d_s" % g].append(new9[:, :, 6 + g])

    st = {k: jnp.stack(v, axis=0) for k, v in acc.items()}
    return (xp.reshape(bp, tp, D_MODEL), xs.reshape(bs, ts, D_MODEL),
            st["ret_p"], st["hg_p"],
            st["wk0_p"], st["wv0_p"], st["wk1_p"], st["wv1_p"], st["wk2_p"], st["wv2_p"],
            st["mk_p"], st["mv_p"],
            st["ret_s"], st["hg_s"],
            st["wk0_s"], st["wv0_s"], st["wk1_s"], st["wv1_s"], st["wk2_s"], st["wv2_s"])
```

```python
import functools
import math

import numpy as np
import jax
import jax.numpy as jnp
from jax import lax
from jax.experimental import pallas as pl
from jax.experimental.pallas import tpu as pltpu

D_MODEL = 1024
DEPTH = 2
PAST_LEN = 2048
D_FF = 2816
BRANCH_W = 512
RET_HEADS = 4
RET_DK = 64
RET_DV = 128
RET_THETA = 10000.0
RET_CHUNK = 128
HG_HEADS = 4
HG_DK = 128
HG_DV = 128
HG_CHUNK = 64
DIL_GROUPS = ((128, 1), (512, 4), (2048, 16))
N_GROUPS = 3
DIL_HEADS = 8
DIL_DH = 64
ROPE_DIM = DIL_DH // 4
ROPE_THETA = 500000.0
DIL_QB = 128
MEM_LEN = 256
X_HEADS = 4
X_DH = D_MODEL // X_HEADS
IN_COLS = 11264
ALPHA = (2 * DEPTH) ** 0.25
EPS = 1e-5
NEG = -1e30
EXP_CLIP = 80.0
F32 = jnp.float32
BF16 = jnp.bfloat16

COL_AQK = 0
COL_AV = 1
COL_AG = 2
COL_BQ = 3
COL_BF = 4
COL_BI = 5
COL_BG = 6
COL_C0 = 7
COL_GATE = 10
MAIN_COLS = 16 * 512
W_IN_CQ, W_IN_CK, W_IN_CV, W_IN_GATE = 3584, 5120, 6656, 8192

VMEM_LIMIT = 56 * 1024 * 1024


def _cparams(*sem):
    return pltpu.CompilerParams(dimension_semantics=sem, vmem_limit_bytes=VMEM_LIMIT)


def _dot(a, b):
    return jnp.dot(a, b, preferred_element_type=F32)


def _dot_nt(a, b):
    return lax.dot_general(a, b, (((1,), (1,)), ((), ())), preferred_element_type=F32)


def _dot_tn(a, b):
    return lax.dot_general(a, b, (((0,), (0,)), ((), ())), preferred_element_type=F32)


def _ln(y, g, b):
    yc = y - jnp.mean(y, -1, keepdims=True)
    var = jnp.mean(yc * yc, -1, keepdims=True)
    return yc * lax.rsqrt(var + EPS) * g + b


def _silu(x):
    return x * jax.nn.sigmoid(x)


def _ffn_kernel(x_ref, wg_ref, wu_ref, wd_ref, g_ref, b_ref, o_ref, ob_ref, acc_ref, xb_ref):
    j = pl.program_id(1)

    @pl.when(j == 0)
    def _():
        acc_ref[...] = jnp.zeros_like(acc_ref)
        xb_ref[...] = x_ref[...].astype(BF16)

    xb = xb_ref[...]
    hg = _dot(xb, wg_ref[...])
    hu = _dot(xb, wu_ref[...])
    hid = (_silu(hg) * hu).astype(BF16)
    acc_ref[...] += _dot(hid, wd_ref[...])

    @pl.when(j == pl.num_programs(1) - 1)
    def _():
        y = _ln(ALPHA * x_ref[...] + 0.5 * acc_ref[...], g_ref[...], b_ref[...])
        o_ref[...] = y
        ob_ref[...] = y.astype(BF16)


def _ffn_ln(x, wg, wu, wd, g, b, tm=512, tf=D_FF // 2):
    n = x.shape[0]
    row = pl.BlockSpec((tm, D_MODEL), lambda i, j: (i, 0))
    return pl.pallas_call(
        _ffn_kernel,
        grid=(n // tm, D_FF // tf),
        in_specs=[
            row,
            pl.BlockSpec((D_MODEL, tf), lambda i, j: (0, j)),
            pl.BlockSpec((D_MODEL, tf), lambda i, j: (0, j)),
            pl.BlockSpec((tf, D_MODEL), lambda i, j: (j, 0)),
            pl.BlockSpec((1, D_MODEL), lambda i, j: (0, 0)),
            pl.BlockSpec((1, D_MODEL), lambda i, j: (0, 0)),
        ],
        out_specs=[row, row],
        out_shape=[jax.ShapeDtypeStruct((n, D_MODEL), F32), jax.ShapeDtypeStruct((n, D_MODEL), BF16)],
        scratch_shapes=[pltpu.VMEM((tm, D_MODEL), F32), pltpu.VMEM((tm, D_MODEL), BF16)],
        compiler_params=_cparams("parallel", "arbitrary"),
        name="ffn_ln",
    )(x, wg, wu, wd, g, b)


def _rot128(y, c, s_lo, s_hi, half):
    return y * c + pltpu.roll(y, 128 - half, 1) * s_lo + pltpu.roll(y, half, 1) * s_hi


def _inproj_kernel(x_ref, w_ref, tab_ref, o_ref, *, ret_blocks, dil_blocks):
    j = pl.program_id(1)
    o_ref[...] = _dot(x_ref[...], w_ref[...])

    def among(blocks):
        hit = j == blocks[0]
        for c in blocks[1:]:
            hit = jnp.logical_or(hit, j == c)
        return hit

    for blocks, t0, half in ((ret_blocks, 0, RET_DK // 2), (dil_blocks, 3, ROPE_DIM // 2)):
        if not blocks:
            continue

        @pl.when(among(blocks))
        def _(t0=t0, half=half):
            for c in range(4):
                sl = slice(c * 128, (c + 1) * 128)
                o_ref[:, sl] = _rot128(o_ref[:, sl], tab_ref[t0], tab_ref[t0 + 1], tab_ref[t0 + 2], half)


def _inproj(x, w, tabs, tm, ret_blocks, dil_blocks):
    n = x.shape[0]
    cols = w.shape[1]
    nt = tabs.shape[1] // tm
    return pl.pallas_call(
        functools.partial(_inproj_kernel, ret_blocks=ret_blocks, dil_blocks=dil_blocks),
        grid=(n // tm, cols // 512),
        in_specs=[
            pl.BlockSpec((tm, D_MODEL), lambda i, j: (i, 0)),
            pl.BlockSpec((D_MODEL, 512), lambda i, j: (0, j)),
            pl.BlockSpec((6, tm, 128), lambda i, j: (0, i % nt, 0)),
        ],
        out_specs=pl.BlockSpec((tm, 512), lambda i, j: (i, j)),
        out_shape=jax.ShapeDtypeStruct((n, cols), F32),
        compiler_params=_cparams("parallel", "arbitrary"),
        name="in_proj_rotary",
    )(x, w, tabs)


def _mm_kernel(x_ref, w_ref, o_ref):
    o_ref[...] = _dot(x_ref[...].astype(BF16), w_ref[...])


def _matmul(x, w, tm, tn=512):
    n, k = x.shape
    m = w.shape[1]
    return pl.pallas_call(
        _mm_kernel,
        grid=(n // tm, m // tn),
        in_specs=[pl.BlockSpec((tm, k), lambda i, j: (i, 0)),
                  pl.BlockSpec((k, tn), lambda i, j: (0, j))],
        out_specs=pl.BlockSpec((tm, tn), lambda i, j: (i, j)),
        out_shape=jax.ShapeDtypeStruct((n, m), F32),
        compiler_params=_cparams("parallel", "arbitrary"),
        name="matmul",
    )(x, w)


def _mm_res_ln_kernel(x_ref, w_ref, r_ref, g_ref, b_ref, o_ref):
    y = _dot(x_ref[...].astype(BF16), w_ref[...])
    o_ref[...] = _ln(ALPHA * r_ref[...] + y, g_ref[...], b_ref[...])


def _matmul_res_ln(x, w, res, g, b, tm):
    n = x.shape[0]
    return pl.pallas_call(
        _mm_res_ln_kernel,
        grid=(n // tm,),
        in_specs=[pl.BlockSpec((tm, D_MODEL), lambda i: (i, 0)),
                  pl.BlockSpec((D_MODEL, D_MODEL), lambda i: (0, 0)),
                  pl.BlockSpec((tm, D_MODEL), lambda i: (i, 0)),
                  pl.BlockSpec((1, D_MODEL), lambda i: (0, 0)),
                  pl.BlockSpec((1, D_MODEL), lambda i: (0, 0))],
        out_specs=pl.BlockSpec((tm, D_MODEL), lambda i: (i, 0)),
        out_shape=jax.ShapeDtypeStruct((n, D_MODEL), F32),
        compiler_params=_cparams("parallel"),
        name="proj_res_ln",
    )(x, w, res, g, b)


def _ret_gammas():
    return [1.0 - 2.0 ** (-5.0 - h) for h in range(RET_HEADS)]


def _ret_kernel(qk_ref, v_ref, s0_ref, din_ref, dq_ref, dk_ref, o_ref, so_ref, r_scr, *, nchunk, dc):
    c = RET_CHUNK

    @pl.when(pl.program_id(1) == 0)
    def _():
        r_scr[...] = s0_ref[0]

    def chunk(ci, carry):
        row = pl.multiple_of(ci * c, c)
        qk = qk_ref[0, pl.ds(row, c), :]
        v = v_ref[0, pl.ds(row, c), :]
        for h in range(RET_HEADS):
            q = qk[:, h * RET_DK:(h + 1) * RET_DK] * (RET_DK ** -0.5)
            k = qk[:, 256 + h * RET_DK:256 + (h + 1) * RET_DK]
            vh = v[:, h * RET_DV:(h + 1) * RET_DV].astype(BF16)
            r = r_scr[h]
            s = _dot_nt(q.astype(BF16), k.astype(BF16)) * din_ref[h]
            o = _dot(s.astype(BF16), vh) + _dot((q * dq_ref[h]).astype(BF16), r.astype(BF16))
            r_scr[h] = r * dc[h] + _dot_tn((k * dk_ref[h]).astype(BF16), vh)
            o_ref[0, pl.ds(row, c), h * RET_DV:(h + 1) * RET_DV] = o
        return carry

    lax.fori_loop(0, nchunk, chunk, 0)

    @pl.when(pl.program_id(1) == pl.num_programs(1) - 1)
    def _():
        so_ref[0] = r_scr[...]


def _retention_prompt(proj3, s0, tt=1024):
    b, t, _ = proj3.shape
    c = RET_CHUNK
    gam = np.array(_ret_gammas(), np.float64)
    i = np.arange(c, dtype=np.float64)
    diff = i[:, None] - i[None, :]
    din = np.where(diff >= 0, gam[:, None, None] ** np.maximum(diff, 0.0), 0.0)
    dq = np.broadcast_to((gam[:, None] ** (i + 1.0))[:, :, None], (RET_HEADS, c, RET_DK))
    dk = np.broadcast_to((gam[:, None] ** (c - 1.0 - i))[:, :, None], (RET_HEADS, c, RET_DK))
    dc = tuple(float(g ** c) for g in gam)
    const = lambda shape: pl.BlockSpec(shape, lambda bi, ti: (0,) * len(shape))
    return pl.pallas_call(
        functools.partial(_ret_kernel, nchunk=tt // c, dc=dc),
        grid=(b, t // tt),
        in_specs=[
            pl.BlockSpec((1, tt, 512), lambda bi, ti: (bi, ti, COL_AQK)),
            pl.BlockSpec((1, tt, 512), lambda bi, ti: (bi, ti, COL_AV)),
            pl.BlockSpec((1, RET_HEADS, RET_DK, RET_DV), lambda bi, ti: (bi, 0, 0, 0)),
            const((RET_HEADS, c, c)), const((RET_HEADS, c, RET_DK)), const((RET_HEADS, c, RET_DK)),
        ],
        out_specs=[
            pl.BlockSpec((1, tt, 512), lambda bi, ti: (bi, ti, 0)),
            pl.BlockSpec((1, RET_HEADS, RET_DK, RET_DV), lambda bi, ti: (bi, 0, 0, 0)),
        ],
        out_shape=[jax.ShapeDtypeStruct((b, t, BRANCH_W), F32),
                   jax.ShapeDtypeStruct((b, RET_HEADS, RET_DK, RET_DV), F32)],
        scratch_shapes=[pltpu.VMEM((RET_HEADS, RET_DK, RET_DV), F32)],
        compiler_params=_cparams("parallel", "arbitrary"),
        name="retention_chunked",
    )(proj3, proj3, s0, jnp.asarray(din, F32), jnp.asarray(dq, F32), jnp.asarray(dk, F32))


def _hgrn_log_f(z, lb):
    log_sig = jnp.minimum(z, 0.0) - jnp.log1p(jnp.exp(-jnp.abs(z)))
    return log_sig + jnp.log1p(lb * jnp.exp(jnp.minimum(-z, EXP_CLIP)))


def _cumsum_rows(g):
    n = g.shape[0]
    row = lax.broadcasted_iota(jnp.int32, g.shape, 0)
    sh = 1
    while sh < n:
        g = g + jnp.where(row >= sh, pltpu.roll(g, sh, 0), 0.0)
        sh *= 2
    return g


def _hgrn_kernel(q_ref, f_ref, i_ref, lb_ref, s0_ref, o_ref, so_ref, s_scr, *, nchunk):
    c = HG_CHUNK
    nslab = c // 8

    @pl.when(pl.program_id(2) == 0)
    def _():
        s_scr[...] = s0_ref[0, 0]

    lb = lb_ref[...]
    sub = 16
    rowi = lax.broadcasted_iota(jnp.int32, (8, HG_DK), 0)
    rowc = lax.broadcasted_iota(jnp.int32, (c, HG_DK), 0)

    def chunk(ci, carry):
        row = pl.multiple_of(ci * c, c)
        qh = _silu(q_ref[0, pl.ds(row, c), :]) * (HG_DK ** -0.5)
        g = _hgrn_log_f(f_ref[0, pl.ds(row, c), :], lb)
        v = i_ref[0, pl.ds(row, c), :]
        b = _cumsum_rows(g)
        ki = 1.0 - jnp.exp(g)
        s_mat = s_scr[...]
        vb = v.astype(BF16)
        o = _dot((qh * jnp.exp(b)).astype(BF16), s_mat.astype(BF16))
        a_blocks = [jnp.zeros((sub, c), F32)]
        for blk in range(1, c // sub):
            lo = blk * sub
            r = b[lo - 1:lo]
            qt = qh[lo:lo + sub] * jnp.exp(b[lo:lo + sub] - r)
            kt = jnp.where(rowc < lo, ki * jnp.exp(jnp.minimum(r - b, 0.0)), 0.0)
            a_blocks.append(_dot_nt(qt.astype(BF16), kt.astype(BF16)))
        o = o + _dot(jnp.concatenate(a_blocks, axis=0).astype(BF16), vb)
        o_sl = [o[8 * j:8 * j + 8] for j in range(nslab)]
        b_sl = [b[8 * j:8 * j + 8] for j in range(nslab)]
        q_sl = [qh[8 * j:8 * j + 8] for j in range(nslab)]
        for s in range(c):
            js = s // 8
            bs, ks, vs = b[s:s + 1], ki[s:s + 1], v[s:s + 1]
            for j in range(js, (s // sub + 1) * (sub // 8)):
                d = b_sl[j] - bs
                if j == js:
                    d = jnp.where(rowi >= s % 8, d, NEG)
                a = jnp.sum(q_sl[j] * ks * jnp.exp(d), axis=-1, keepdims=True)
                o_sl[j] = o_sl[j] + a * vs
        o_ref[0, pl.ds(row, c), :] = jnp.concatenate(o_sl, axis=0)
        b_end = b[c - 1:c]
        decay_col = jnp.broadcast_to(jnp.exp(b_end), (HG_DK, HG_DK)).T
        s_scr[...] = s_mat * decay_col + _dot_tn((ki * jnp.exp(b_end - b)).astype(BF16), v.astype(BF16))
        return carry

    lax.fori_loop(0, nchunk, chunk, 0)

    @pl.when(pl.program_id(2) == pl.num_programs(2) - 1)
    def _():
        so_ref[0, 0] = s_scr[...]


def _hgrn_prompt(proj3, lb, s0, tt=1024):
    b, t, _ = proj3.shape
    col = lambda base: (lambda bi, h, ti: (bi, ti, base * 4 + h))
    return pl.pallas_call(
        functools.partial(_hgrn_kernel, nchunk=tt // HG_CHUNK),
        grid=(b, HG_HEADS, t // tt),
        in_specs=[
            pl.BlockSpec((1, tt, 128), col(COL_BQ)),
            pl.BlockSpec((1, tt, 128), col(COL_BF)),
            pl.BlockSpec((1, tt, 128), col(COL_BI)),
            pl.BlockSpec((1, 128), lambda bi, h, ti: (0, h)),
            pl.BlockSpec((1, 1, HG_DK, HG_DV), lambda bi, h, ti: (bi, h, 0, 0)),
        ],
        out_specs=[
            pl.BlockSpec((1, tt, 128), lambda bi, h, ti: (bi, ti, h)),
            pl.BlockSpec((1, 1, HG_DK, HG_DV), lambda bi, h, ti: (bi, h, 0, 0)),
        ],
        out_shape=[jax.ShapeDtypeStruct((b, t, BRANCH_W), F32),
                   jax.ShapeDtypeStruct((b, HG_HEADS, HG_DK, HG_DV), F32)],
        scratch_shapes=[pltpu.VMEM((HG_DK, HG_DV), F32)],
        compiler_params=_cparams("parallel", "parallel", "arbitrary"),
        name="hgrn2_chunked",
    )(proj3, proj3, proj3, lb, s0)


def _dil_kernel(q_ref, kc_ref, kp_ref, vc_ref, vp_ref, o_ref, lse_ref, s_scr, p_scr):
    i = pl.program_id(2)
    qb = DIL_QB
    npair = DIL_HEADS // 2
    pairs = [slice(c * 128, (c + 1) * 128) for c in range(npair)]
    first = lax.broadcasted_iota(jnp.int32, (qb, 128), 1) < DIL_DH
    for c, psl in enumerate(pairs):
        q = q_ref[:, psl] * (DIL_DH ** -0.5)
        kp = kp_ref[:, psl].astype(BF16)
        kc = kc_ref[:, psl].astype(BF16)
        for half in range(2):
            qm = jnp.where(first if half == 0 else jnp.logical_not(first), q, 0.0).astype(BF16)
            s_scr[2 * c + half, :, 0:qb] = _dot_nt(qm, kp)
            s_scr[2 * c + half, :, qb:2 * qb] = _dot_nt(qm, kc)
    t = lax.broadcasted_iota(jnp.int32, (qb, qb), 0)
    s = lax.broadcasted_iota(jnp.int32, (qb, qb), 1)
    in_prev = jnp.logical_and(s >= t, i > 0)
    in_cur = s <= t
    for c, psl in enumerate(pairs):
        lses = []
        for half in range(2):
            h = 2 * c + half
            sp = jnp.where(in_prev, s_scr[h, :, 0:qb], NEG)
            sc = jnp.where(in_cur, s_scr[h, :, qb:2 * qb], NEG)
            m = jnp.max(jnp.maximum(sp, sc), -1, keepdims=True)
            pp = jnp.exp(sp - m)
            pc = jnp.exp(sc - m)
            l = jnp.sum(pp + pc, -1, keepdims=True)
            p_scr[h, :, 0:qb] = (pp / l).astype(BF16)
            p_scr[h, :, qb:2 * qb] = (pc / l).astype(BF16)
            lses.append(m + jnp.log(l))
        lse_ref[:, psl] = jnp.where(first, lses[0], lses[1])
    for c, psl in enumerate(pairs):
        vp = vp_ref[:, psl].astype(BF16)
        vc = vc_ref[:, psl].astype(BF16)
        o2 = [_dot(p_scr[2 * c + half, :, 0:qb], vp) + _dot(p_scr[2 * c + half, :, qb:2 * qb], vc)
              for half in range(2)]
        o_ref[:, psl] = jnp.where(first, o2[0], o2[1])


def _dilated_prompt(qkv4, col0, g):
    b, dil, tr, _ = qkv4.shape
    assert dil == DIL_GROUPS[g][1] and tr % DIL_QB == 0
    cur = lambda c: (lambda bi, r, i: (bi, r, i, c))
    prev = lambda c: (lambda bi, r, i: (bi, r, jnp.maximum(i - 1, 0), c))
    blk = (None, None, DIL_QB, 512)
    oblk = (None, DIL_QB, 512)
    o, lse = pl.pallas_call(
        _dil_kernel,
        grid=(b, dil, tr // DIL_QB),
        in_specs=[pl.BlockSpec(blk, cur(col0)),
                  pl.BlockSpec(blk, cur(col0 + 1)), pl.BlockSpec(blk, prev(col0 + 1)),
                  pl.BlockSpec(blk, cur(col0 + 2)), pl.BlockSpec(blk, prev(col0 + 2))],
        out_specs=[pl.BlockSpec(oblk, lambda bi, r, i: (bi, i, r)),
                   pl.BlockSpec(oblk, lambda bi, r, i: (bi, i, r))],
        out_shape=[jax.ShapeDtypeStruct((b, tr, dil * 512), F32),
                   jax.ShapeDtypeStruct((b, tr, dil * 512), F32)],
        scratch_shapes=[pltpu.VMEM((DIL_HEADS, DIL_QB, 2 * DIL_QB), F32),
                        pltpu.VMEM((DIL_HEADS, DIL_QB, 2 * DIL_QB), BF16)],
        compiler_params=_cparams("parallel", "parallel", "arbitrary"),
        name="dilated_attn_%d" % g,
    )(qkv4, qkv4, qkv4, qkv4, qkv4)
    return o.reshape(b * tr * dil, 512), lse.reshape(b * tr * dil, 512)


def _head_norm(x, centre):
    parts = []
    for h in range(4):
        xh = x[:, h * 128:(h + 1) * 128]
        if centre:
            xh = xh - jnp.mean(xh, -1, keepdims=True)
        parts.append(xh * lax.rsqrt(jnp.mean(xh * xh, -1, keepdims=True) + EPS))
    return jnp.concatenate(parts, axis=1)


def _merge_kernel(h_ref, oa_ref, ag_ref, ob_ref, bg_ref, oc_ref, g0_ref, g1_ref, g2_ref,
                  gn_ref, hn_ref, wb_ref, wo_ref, lg_ref, lb_ref, o_ref):
    oa = _head_norm(oa_ref[...], True) * gn_ref[...] * _silu(ag_ref[...])
    ob = _head_norm(ob_ref[...], False) * hn_ref[...] * jax.nn.sigmoid(bg_ref[...])
    merged = jax.nn.sigmoid(g0_ref[...]) * _dot(oa.astype(BF16), wb_ref[0])
    merged += jax.nn.sigmoid(g1_ref[...]) * _dot(ob.astype(BF16), wb_ref[1])
    merged += jax.nn.sigmoid(g2_ref[...]) * _dot(oc_ref[...].astype(BF16), wb_ref[2])
    mix = _dot(merged.astype(BF16), wo_ref[...])
    o_ref[...] = _ln(ALPHA * h_ref[...] + mix, lg_ref[...], lb_ref[...])


def _merge(h, proj, oa, ob, oc, gn, hn, wb, wo, lg, lb, tm=256):
    n = h.shape[0]
    row512 = lambda cb: pl.BlockSpec((tm, 512), lambda i: (i, cb))
    row1024 = lambda cb: pl.BlockSpec((tm, 1024), lambda i: (i, cb))
    const = lambda shape: pl.BlockSpec(shape, lambda i: (0,) * len(shape))
    return pl.pallas_call(
        _merge_kernel,
        grid=(n // tm,),
        in_specs=[row1024(0), row512(0), row512(COL_AG), row512(0), row512(COL_BG), row512(0),
                  row1024(COL_GATE // 2), row1024(COL_GATE // 2 + 1), row1024(COL_GATE // 2 + 2),
                  const((1, 512)), const((1, 512)), const((3, 512, D_MODEL)), const((D_MODEL, D_MODEL)),
                  const((1, D_MODEL)), const((1, D_MODEL))],
        out_specs=row1024(0),
        out_shape=jax.ShapeDtypeStruct((n, D_MODEL), F32),
        compiler_params=_cparams("parallel"),
        name="branch_merge",
    )(h, oa, proj, ob, proj, oc, proj, proj, proj, gn, hn, wb, wo, lg, lb)


def _lse_merge_kernel(o0, o1, o2, l0, l1, l2, oc_ref):
    ls = [l0[...], l1[...], l2[...]]
    m = jnp.maximum(jnp.maximum(ls[0], ls[1]), ls[2])
    es = [jnp.exp(x - m) for x in ls]
    den = es[0] + es[1] + es[2]
    oc_ref[...] = (es[0] / den) * o0[...] + (es[1] / den) * o1[...] + (es[2] / den) * o2[...]


def _lse_merge(os_, ls_, tm=1024):
    n = os_[0].shape[0]
    spec = pl.BlockSpec((tm, 512), lambda i: (i, 0))
    return pl.pallas_call(
        _lse_merge_kernel,
        grid=(n // tm,),
        in_specs=[spec] * 6,
        out_specs=spec,
        out_shape=jax.ShapeDtypeStruct((n, 512), F32),
        compiler_params=_cparams("parallel"),
        name="group_merge",
    )(*os_, *ls_)


def _xattn_block_kernel(h_ref, wq_ref, k_ref, v_ref, wo_ref, g_ref, b_ref, o_ref):
    h = h_ref[0]
    q = _dot(h.astype(BF16), wq_ref[...])
    outs = []
    for hd in range(X_HEADS):
        sl = slice(hd * X_DH, (hd + 1) * X_DH)
        s = _dot_nt(q[:, sl].astype(BF16), k_ref[0, :, sl]) * (X_DH ** -0.5)
        m = jnp.max(s, -1, keepdims=True)
        p = jnp.exp(s - m)
        p = p / jnp.sum(p, -1, keepdims=True)
        outs.append(_dot(p.astype(BF16), v_ref[0, :, sl]).astype(BF16))
    y = _dot(jnp.concatenate(outs, axis=1), wo_ref[...])
    o_ref[0] = _ln(ALPHA * h + y, g_ref[...], b_ref[...])


def _xattn_block(h3, wq, mk, mv, wo, g, b, tq=512):
    bsz, t, _ = h3.shape
    row = pl.BlockSpec((1, tq, D_MODEL), lambda bi, ti: (bi, ti, 0))
    mem = pl.BlockSpec((1, MEM_LEN, D_MODEL), lambda bi, ti: (bi, 0, 0))
    wspec = pl.BlockSpec((D_MODEL, D_MODEL), lambda bi, ti: (0, 0))
    vec = pl.BlockSpec((1, D_MODEL), lambda bi, ti: (0, 0))
    return pl.pallas_call(
        _xattn_block_kernel,
        grid=(bsz, t // tq),
        in_specs=[row, wspec, mem, mem, wspec, vec, vec],
        out_specs=row,
        out_shape=jax.ShapeDtypeStruct((bsz, t, D_MODEL), F32),
        compiler_params=_cparams("parallel", "arbitrary"),
        name="cross_attn_block",
    )(h3, wq, mk, mv, wo, g, b)


def _xattn_s_kernel(q_ref, k_ref, v_ref, o_ref, *, bb):
    nq = q_ref.shape[1]
    nk = MEM_LEN * X_HEADS
    same_head = (lax.broadcasted_iota(jnp.int32, (nq, nk), 0) % X_HEADS
                 == lax.broadcasted_iota(jnp.int32, (nq, nk), 1) % X_HEADS)
    for bi in range(bb):
        k2 = k_ref[bi].reshape(nk, X_DH).astype(BF16)
        v2 = v_ref[bi].reshape(nk, X_DH).astype(BF16)
        s = _dot_nt(q_ref[bi].astype(BF16), k2) * (X_DH ** -0.5)
        s = jnp.where(same_head, s, NEG)
        m = jnp.max(s, -1, keepdims=True)
        p = jnp.exp(s - m)
        p = p / jnp.sum(p, -1, keepdims=True)
        o_ref[bi] = _dot(p.astype(BF16), v2)


def _xattn_sample(q3, mk5, mv5, layer, bb=2):
    b, nq, _ = q3.shape
    mem_spec = pl.BlockSpec((None, bb, MEM_LEN, X_HEADS, X_DH), lambda bi: (layer, bi, 0, 0, 0))
    row = pl.BlockSpec((bb, nq, X_DH), lambda bi: (bi, 0, 0))
    return pl.pallas_call(
        functools.partial(_xattn_s_kernel, bb=bb),
        grid=(b // bb,),
        in_specs=[row, mem_spec, mem_spec],
        out_specs=row,
        out_shape=jax.ShapeDtypeStruct((b, nq, X_DH), F32),
        compiler_params=_cparams("parallel"),
        name="cross_attn_step",
    )(q3, mk5, mv5)


def _pick_col(tile, onehot):
    return jnp.sum(jnp.where(onehot, tile, 0.0), axis=1, keepdims=True)


def _pick_row(slab, sub_hit):
    return jnp.sum(jnp.where(sub_hit, slab, 0.0), axis=0, keepdims=True)


def _ret_s_kernel(q_ref, k_ref, v_ref, s_ref, o_ref, so_ref, qt_scr, kt_scr, *, nt, bb):
    j = pl.program_id(0)
    nb = q_ref.shape[1]

    @pl.when(j == 0)
    def _():
        for t in range(nt):
            for c in range(2):
                sl = slice(c * 128, (c + 1) * 128)
                qt_scr[t, sl, :] = (q_ref[t, :, sl] * (RET_DK ** -0.5)).T
                kt_scr[t, sl, :] = k_ref[t, :, sl].T

    lane = lax.broadcasted_iota(jnp.int32, (RET_DK, nb), 1)
    sub = lax.broadcasted_iota(jnp.int32, (8, 128), 0)
    o_ref[...] = jnp.zeros_like(o_ref)
    gam = _ret_gammas()

    def body(bi, carry):
        bg = j * bb + bi
        onehot = lane == bg
        row_g = pl.multiple_of((bg // 8) * 8, 8)
        row_l = pl.multiple_of((bi // 8) * 8, 8)
        sub_hit = sub == bi % 8
        for h in range(RET_HEADS):
            r = s_ref[bi, h]
            for t in range(nt):
                ksl = slice(h * RET_DK, (h + 1) * RET_DK)
                qc = _pick_col(qt_scr[t, ksl, :], onehot)
                kc = _pick_col(kt_scr[t, ksl, :], onehot)
                vsl = slice(h * RET_DV, (h + 1) * RET_DV)
                vrow = _pick_row(v_ref[t, pl.ds(row_g, 8), vsl], sub_hit)
                r = r * gam[h] + kc * vrow
                orow = jnp.sum(r * qc, axis=0, keepdims=True)
                o_ref[t, pl.ds(row_l, 8), vsl] = jnp.where(sub_hit, orow, o_ref[t, pl.ds(row_l, 8), vsl])
            so_ref[bi, h] = r
        return carry

    lax.fori_loop(0, bb, body, 0)


def _retention_sample(q_t, k_t, v_t, state, layer, bb=16):
    nt, nb, _ = q_t.shape
    full = lambda w: pl.BlockSpec((nt, nb, w), lambda j: (0, 0, 0))
    return pl.pallas_call(
        functools.partial(_ret_s_kernel, nt=nt, bb=bb),
        grid=(nb // bb,),
        in_specs=[full(256), full(256), full(512),
                  pl.BlockSpec((None, bb, RET_HEADS, RET_DK, RET_DV), lambda j: (layer, j, 0, 0, 0))],
        out_specs=[pl.BlockSpec((nt, bb, 512), lambda j: (0, j, 0)),
                   pl.BlockSpec((bb, RET_HEADS, RET_DK, RET_DV), lambda j: (j, 0, 0, 0))],
        out_shape=[jax.ShapeDtypeStruct((nt, nb, 512), F32),
                   jax.ShapeDtypeStruct((nb, RET_HEADS, RET_DK, RET_DV), F32)],
        scratch_shapes=[pltpu.VMEM((nt, 256, nb), F32), pltpu.VMEM((nt, 256, nb), F32)],
        compiler_params=_cparams("arbitrary"),
        name="retention_step",
    )(q_t, k_t, v_t, state)


def _hgrn_s_kernel(q_ref, f_ref, i_ref, lb_ref, s_ref, o_ref, so_ref, qt_scr, ft_scr, kt_scr, *, nt, bb):
    j = pl.program_id(0)
    nb = q_ref.shape[1]

    @pl.when(j == 0)
    def _():
        for t in range(nt):
            for h in range(HG_HEADS):
                sl = slice(h * HG_DK, (h + 1) * HG_DK)
                f = jnp.exp(_hgrn_log_f(f_ref[t, :, sl], lb_ref[:, sl]))
                qt_scr[t, sl, :] = (_silu(q_ref[t, :, sl]) * (HG_DK ** -0.5)).T
                ft_scr[t, sl, :] = f.T
                kt_scr[t, sl, :] = (1.0 - f).T

    lane = lax.broadcasted_iota(jnp.int32, (HG_DK, nb), 1)
    sub = lax.broadcasted_iota(jnp.int32, (8, 128), 0)
    o_ref[...] = jnp.zeros_like(o_ref)

    def body(bi, carry):
        bg = j * bb + bi
        onehot = lane == bg
        row_g = pl.multiple_of((bg // 8) * 8, 8)
        row_l = pl.multiple_of((bi // 8) * 8, 8)
        sub_hit = sub == bi % 8
        for h in range(HG_HEADS):
            sl = slice(h * HG_DK, (h + 1) * HG_DK)
            s_mat = s_ref[bi, h]
            for t in range(nt):
                qc = _pick_col(qt_scr[t, sl, :], onehot)
                fc = _pick_col(ft_scr[t, sl, :], onehot)
                kc = _pick_col(kt_scr[t, sl, :], onehot)
                vrow = _pick_row(i_ref[t, pl.ds(row_g, 8), sl], sub_hit)
                s_mat = s_mat * fc + kc * vrow
                orow = jnp.sum(s_mat * qc, axis=0, keepdims=True)
                o_ref[t, pl.ds(row_l, 8), sl] = jnp.where(sub_hit, orow, o_ref[t, pl.ds(row_l, 8), sl])
            so_ref[bi, h] = s_mat
        return carry

    lax.fori_loop(0, bb, body, 0)


def _hgrn_sample(q_t, f_t, i_t, lb, state, layer, bb=16):
    nt, nb, _ = q_t.shape
    full = pl.BlockSpec((nt, nb, 512), lambda j: (0, 0, 0))
    return pl.pallas_call(
        functools.partial(_hgrn_s_kernel, nt=nt, bb=bb),
        grid=(nb // bb,),
        in_specs=[full, full, full, pl.BlockSpec((1, 512), lambda j: (0, 0)),
                  pl.BlockSpec((None, bb, HG_HEADS, HG_DK, HG_DV), lambda j: (layer, j, 0, 0, 0))],
        out_specs=[pl.BlockSpec((nt, bb, 512), lambda j: (0, j, 0)),
                   pl.BlockSpec((bb, HG_HEADS, HG_DK, HG_DV), lambda j: (j, 0, 0, 0))],
        out_shape=[jax.ShapeDtypeStruct((nt, nb, 512), F32),
                   jax.ShapeDtypeStruct((nb, HG_HEADS, HG_DK, HG_DV), F32)],
        scratch_shapes=[pltpu.VMEM((nt, 512, nb), F32)] * 3,
        compiler_params=_cparams("arbitrary"),
        name="hgrn2_step",
    )(q_t, f_t, i_t, lb, state)


def _dil_s_kernel(new_ref, k0_ref, v0_ref, k1_ref, v1_ref, k2_ref, v2_ref, o_ref, qpad_scr, opad_scr, *, nt):
    caches = ((k0_ref, v0_ref), (k1_ref, v1_ref), (k2_ref, v2_ref))
    scale = DIL_DH ** -0.5
    lane = lax.broadcasted_iota(jnp.int32, (DIL_DH, 128), 1)
    sub8 = lax.broadcasted_iota(jnp.int32, (DIL_HEADS, 128), 0)
    lane8 = lax.broadcasted_iota(jnp.int32, (DIL_HEADS, 128), 1)
    col_id = lambda g, i, h: (g * nt + i) * DIL_HEADS + h

    qpad_scr[...] = jnp.zeros_like(qpad_scr)
    opad_scr[...] = jnp.zeros_like(opad_scr)
    for g in range(N_GROUPS):
        for i in range(nt):
            r0 = col_id(g, i, 0)
            qpad_scr[r0:r0 + DIL_HEADS, 0:DIL_DH] = new_ref[0, i, g]
    qt = qpad_scr[...].T[0:DIL_DH]
    qcol = lambda g, i, h: qt[:, col_id(g, i, h):col_id(g, i, h) + 1]

    oc = jnp.zeros((DIL_DH, 128), F32)
    m_all = [[None] * nt for _ in range(N_GROUPS)]
    l_all = [[None] * nt for _ in range(N_GROUPS)]
    pn_all = [[None] * nt for _ in range(N_GROUPS)]
    for g in range(N_GROUPS):
        kc_ref, vc_ref = caches[g]
        dil = DIL_GROUPS[g][1]
        ntile = kc_ref.shape[-1] // 128
        q_t = [new_ref[0, i, g] for i in range(nt)]
        kn = [new_ref[0, j, 3 + g] for j in range(nt)]
        new_score = lambda i, j: jnp.sum(q_t[i] * kn[j], axis=-1, keepdims=True) * scale
        if dil == 1:
            for i in range(nt):
                s = jnp.zeros((DIL_HEADS, 128), F32)
                for h in range(DIL_HEADS):
                    row = jnp.sum(kc_ref[0, h] * qcol(g, i, h), axis=0, keepdims=True)
                    s = jnp.where(sub8 == h, row, s)
                s = jnp.where(lane8 >= i, s * scale, NEG)
                sn = [new_score(i, j) for j in range(i + 1)]
                m = jnp.max(s, -1, keepdims=True)
                for x in sn:
                    m = jnp.maximum(m, x)
                p = jnp.exp(s - m)
                pn = [jnp.exp(x - m) for x in sn]
                l = jnp.sum(p, -1, keepdims=True)
                for x in pn:
                    l = l + x
                for h in range(DIL_HEADS):
                    col = jnp.sum(vc_ref[0, h] * p[h:h + 1, :], axis=1, keepdims=True)
                    oc = jnp.where(lane == col_id(g, i, h), col, oc)
                m_all[g][i], l_all[g][i], pn_all[g][i] = m, l, list(zip(pn, range(i + 1)))
        else:
            assert nt <= dil and 128 % dil == 0
            cls8 = [(lane8 & (dil - 1)) == i for i in range(nt)]
            cls = [(lane & (dil - 1)) == i for i in range(nt)]
            s_t = [jnp.zeros((DIL_HEADS, 128), F32) for _ in range(ntile)]
            for h in range(DIL_HEADS):
                qsel = jnp.zeros((DIL_DH, 128), F32)
                for i in range(nt):
                    qsel = jnp.where(cls[i], qcol(g, i, h), qsel)
                for t in range(ntile):
                    row = jnp.sum(kc_ref[0, h, :, t * 128:(t + 1) * 128] * qsel, axis=0, keepdims=True)
                    s_t[t] = jnp.where(sub8 == h, row, s_t[t])
            s_t = [s * scale for s in s_t]
            smax = s_t[0]
            for t in range(1, ntile):
                smax = jnp.maximum(smax, s_t[t])
            m_tile = jnp.zeros((DIL_HEADS, 128), F32)
            owned = cls8[0]
            for i in range(nt):
                sn = new_score(i, i)
                m = jnp.maximum(jnp.max(jnp.where(cls8[i], smax, NEG), -1, keepdims=True), sn)
                m_all[g][i] = m
                pn_all[g][i] = [(jnp.exp(sn - m), i)]
                m_tile = jnp.where(cls8[i], m, m_tile)
                owned = jnp.logical_or(owned, cls8[i])
            p_t = [jnp.exp(jnp.where(owned, s - m_tile, NEG)) for s in s_t]
            psum = p_t[0]
            for t in range(1, ntile):
                psum = psum + p_t[t]
            for i in range(nt):
                l_all[g][i] = jnp.sum(jnp.where(cls8[i], psum, 0.0), -1, keepdims=True) + pn_all[g][i][0][0]
            for h in range(DIL_HEADS):
                acc = jnp.zeros((DIL_DH, 128), F32)
                for t in range(ntile):
                    acc = acc + vc_ref[0, h, :, t * 128:(t + 1) * 128] * p_t[t][h:h + 1, :]
                for i in range(nt):
                    col = jnp.sum(jnp.where(cls[i], acc, 0.0), axis=1, keepdims=True)
                    oc = jnp.where(lane == col_id(g, i, h), col, oc)

    opad_scr[0:DIL_DH, :] = oc
    ot = opad_scr[...].T
    for i in range(nt):
        outs, lses = [], []
        for g in range(N_GROUPS):
            r0 = col_id(g, i, 0)
            o = ot[r0:r0 + DIL_HEADS, 0:DIL_DH]
            for pj, j in pn_all[g][i]:
                o = o + pj * new_ref[0, j, 6 + g]
            outs.append(o / l_all[g][i])
            lses.append(m_all[g][i] + jnp.log(l_all[g][i]))
        m = jnp.maximum(jnp.maximum(lses[0], lses[1]), lses[2])
        es = [jnp.exp(x - m) for x in lses]
        den = es[0] + es[1] + es[2]
        o_ref[0, i] = (es[0] / den) * outs[0] + (es[1] / den) * outs[1] + (es[2] / den) * outs[2]


def _dilated_sample(new9, caches, layer):
    nb, nt = new9.shape[:2]
    views, specs = [], []
    for g, (kbuf, vbuf) in enumerate(caches):
        w = kbuf.shape[2]
        assert w == DIL_GROUPS[g][0] and w // DIL_GROUPS[g][1] == 128
        for buf in (kbuf, vbuf):
            views.append(jnp.transpose(buf, (0, 1, 3, 4, 2)))
            specs.append(pl.BlockSpec((None, 1, DIL_HEADS, DIL_DH, w), lambda j: (layer, j, 0, 0, 0)))
    return pl.pallas_call(
        functools.partial(_dil_s_kernel, nt=nt),
        grid=(nb,),
        in_specs=[pl.BlockSpec((1, nt, 9, DIL_HEADS, DIL_DH), lambda j: (j, 0, 0, 0, 0))] + specs,
        out_specs=pl.BlockSpec((1, nt, DIL_HEADS, DIL_DH), lambda j: (j, 0, 0, 0)),
        out_shape=jax.ShapeDtypeStruct((nb, nt, DIL_HEADS, DIL_DH), F32),
        scratch_shapes=[pltpu.VMEM((128, 128), F32), pltpu.VMEM((128, 128), F32)],
        compiler_params=_cparams("parallel"),
        name="dilated_attn_step",
    )(new9, *views)


def _rotary_tables(pos):
    pos = jnp.asarray(np.asarray(pos), jnp.int32)
    lane = np.arange(128) % 64
    out = []
    for rot_dim, theta in ((RET_DK, RET_THETA), (ROPE_DIM, ROPE_THETA)):
        half = rot_dim // 2
        inv = jnp.power(jnp.float32(theta), -jnp.arange(half, dtype=F32) / half)
        ang = pos.astype(F32)[:, None] * inv[None, :]
        idx = lane % half
        cos = jnp.where(lane < rot_dim, jnp.cos(ang)[:, idx], 1.0)
        sin = jnp.sin(ang)[:, idx]
        out += [cos, jnp.where(lane < half, -sin, 0.0),
                jnp.where((lane >= half) & (lane < rot_dim), sin, 0.0)]
    return jnp.stack(out, 0).astype(F32)


def _hgrn_lower_bounds(raw):
    p = jax.nn.softmax(raw.astype(F32), axis=0)
    return jnp.cumsum(p, axis=0) - p[0:1]


def _row(v):
    return v.reshape(1, -1)


def kernel(x_prompt, x_sample, mem_prompt, state_ret, state_hgrn, cache_win_k0, cache_win_v0, cache_win_k1, cache_win_v1, cache_win_k2, cache_win_v2, cache_mem_k, cache_mem_v, ln_g, ln_b, ffn_w_gate, ffn_w_up, ffn_w_down, w_in, ret_gn_g, hgrn_lb_raw, hgrn_norm_g, w_branch, w_out, xattn_w_q, xattn_w_k, xattn_w_v, xattn_w_o):
    bp, tp, _ = x_prompt.shape
    bs, ts, _ = x_sample.shape
    np_, ns = bp * tp, bs * ts
    lb_all = _hgrn_lower_bounds(hgrn_lb_raw)
    tab_p = _rotary_tables(np.arange(tp))
    tab_res = [tab_p] + [_rotary_tables(np.arange(tp).reshape(tp // dil, dil).T.reshape(-1))
                         for _, dil in DIL_GROUPS[1:]]
    tm_s = ns
    tab_s = _rotary_tables(PAST_LEN + (np.arange(tm_s) % ts))
    caches = ((cache_win_k0, cache_win_v0), (cache_win_k1, cache_win_v1), (cache_win_k2, cache_win_v2))
    mem2 = mem_prompt.reshape(bp * MEM_LEN, D_MODEL)

    xp = x_prompt.reshape(np_, D_MODEL)
    xs = x_sample.reshape(ns, D_MODEL)
    tm_p = 1024
    acc = {k: [] for k in ("ret_p", "hg_p", "mk_p", "mv_p", "ret_s", "hg_s")}
    for g in range(N_GROUPS):
        for k in ("wk%d_p", "wv%d_p", "wk%d_s", "wv%d_s"):
            acc[k % g] = []

    for l in range(DEPTH):
        wg = ffn_w_gate[l].astype(BF16)
        wu = ffn_w_up[l].astype(BF16)
        wd = ffn_w_down[l].astype(BF16)
        win = w_in[l].astype(BF16)
        w_main = jnp.concatenate(
            [win[:, :W_IN_CQ], win[:, W_IN_CQ:W_IN_CQ + 512], win[:, W_IN_CK:W_IN_CK + 512],
             win[:, W_IN_CV:W_IN_CV + 512], win[:, W_IN_GATE:]], axis=1)
        w_grp = [None] + [jnp.concatenate([win[:, o + g * 512:o + (g + 1) * 512]
                                           for o in (W_IN_CQ, W_IN_CK, W_IN_CV)], axis=1)
                          for g in range(1, N_GROUPS)]
        wb = w_branch[l].astype(BF16)
        wo = w_out[l].astype(BF16)
        wq = xattn_w_q[l].astype(BF16)
        wk = xattn_w_k[l].astype(BF16)
        wv = xattn_w_v[l].astype(BF16)
        wxo = xattn_w_o[l].astype(BF16)
        lng = [_row(ln_g[l, i]) for i in range(4)]
        lnb = [_row(ln_b[l, i]) for i in range(4)]
        gn, hn, lb = _row(ret_gn_g[l]), _row(hgrn_norm_g[l]), _row(lb_all[l])

        mk = _matmul(mem2, wk, tm=min(512, bp * MEM_LEN))
        mv = _matmul(mem2, wv, tm=min(512, bp * MEM_LEN))
        h, hb = _ffn_ln(xp, wg[0], wu[0], wd[0], lng[0], lnb[0])
        proj = _inproj(hb, w_main, tab_p, tm_p, (COL_AQK,), (COL_C0, COL_C0 + 1))
        proj3 = proj.reshape(bp, tp, MAIN_COLS)
        oa, rp = _retention_prompt(proj3, jnp.zeros((bp, RET_HEADS, RET_DK, RET_DV), F32))
        ob, gp = _hgrn_prompt(proj3, lb, jnp.zeros((bp, HG_HEADS, HG_DK, HG_DV), F32))
        dres = [_dilated_prompt(proj.reshape(bp, 1, tp, MAIN_COLS), COL_C0, 0)]
        pgrp = [None]
        for g in range(1, N_GROUPS):
            dil = DIL_GROUPS[g][1]
            h_res = hb.reshape(bp, tp // dil, dil, D_MODEL).transpose(0, 2, 1, 3).reshape(np_, D_MODEL)
            pg = _inproj(h_res, w_grp[g], tab_res[g], tm_p, (), (0, 1)).reshape(bp, dil, tp // dil, 1536)
            pgrp.append(pg)
            dres.append(_dilated_prompt(pg, 0, g))
        oc = _lse_merge([d[0] for d in dres], [d[1] for d in dres])
        h = _merge(h, proj, oa.reshape(np_, 512), ob.reshape(np_, 512), oc, gn, hn, wb, wo, lng[1], lnb[1])
        h = _xattn_block(h.reshape(bp, tp, D_MODEL), wq, mk.astype(BF16).reshape(bp, MEM_LEN, D_MODEL),
                         mv.astype(BF16).reshape(bp, MEM_LEN, D_MODEL), wxo, lng[2], lnb[2])
        xp, _ = _ffn_ln(h.reshape(np_, D_MODEL), wg[1], wu[1], wd[1], lng[3], lnb[3])

        acc["ret_p"].append(rp)
        acc["hg_p"].append(gp)
        acc["mk_p"].append(mk.reshape(bp, MEM_LEN, X_HEADS, X_DH))
        acc["mv_p"].append(mv.reshape(bp, MEM_LEN, X_HEADS, X_DH))
        for g, (window, dil) in enumerate(DIL_GROUPS):
            keep = min(window, tp)
            assert keep % dil == 0
            for name, off in (("wk%d_p", 1), ("wv%d_p", 2)):
                if g == 0:
                    c0 = (COL_C0 + off) * 512
                    rows = proj3[:, tp - keep:, c0:c0 + 512]
                else:
                    rows = pgrp[g][:, :, (tp - keep) // dil:, off * 512:(off + 1) * 512]
                    rows = rows.transpose(0, 2, 1, 3)
                acc[name % g].append(rows.reshape(bp, keep, DIL_HEADS, DIL_DH))

        h, hb = _ffn_ln(xs, wg[0], wu[0], wd[0], lng[0], lnb[0])
        proj = _inproj(hb, w_main, tab_s, tm_s, (COL_AQK,), (COL_C0, COL_C0 + 1))
        proj3 = proj.reshape(bs, ts, MAIN_COLS)
        tmaj = lambda lo, hi: proj3[:, :, lo:hi].transpose(1, 0, 2)
        oa, rs = _retention_sample(tmaj(0, 256), tmaj(256, 512), tmaj(512, 1024), state_ret, l)
        ob, gs = _hgrn_sample(tmaj(COL_BQ * 512, COL_BF * 512), tmaj(COL_BF * 512, COL_BI * 512),
                              tmaj(COL_BI * 512, COL_BG * 512), lb, state_hgrn, l)
        oa = oa.transpose(1, 0, 2).reshape(ns, 512)
        ob = ob.transpose(1, 0, 2).reshape(ns, 512)
        qkv_s = [proj[:, COL_C0 * 512:(COL_C0 + 3) * 512]]
        qkv_s += [_inproj(hb, w_grp[g], tab_s, tm_s, (), (0, 1)) for g in range(1, N_GROUPS)]
        new9 = jnp.stack([qkv_s[g][:, c * 512:(c + 1) * 512].reshape(bs, ts, DIL_HEADS, DIL_DH)
                          for c in range(3) for g in range(N_GROUPS)], axis=2)
        oc = _dilated_sample(new9, caches, l).reshape(ns, 512)
        h = _merge(h, proj, oa, ob, oc, gn, hn, wb, wo, lng[1], lnb[1])
        q = _matmul(h, wq, tm=tm_s)
        xo = _xattn_sample(q.reshape(bs, ts * X_HEADS, X_DH), cache_mem_k, cache_mem_v, l)
        h = _matmul_res_ln(xo.reshape(ns, D_MODEL), wxo, h, lng[2], lnb[2], tm=tm_s)
        xs, _ = _ffn_ln(h, wg[1], wu[1], wd[1], lng[3], lnb[3])

        acc["ret_s"].append(rs)
        acc["hg_s"].append(gs)
        for g in range(N_GROUPS):
            acc["wk%d_s" % g].append(new9[:, :, 3 + g])
            acc["wv%d_s" % g].append(new9[:, :, 6 + g])

    st = {k: jnp.stack(v, axis=0) for k, v in acc.items()}
    return (xp.reshape(bp, tp, D_MODEL), xs.reshape(bs, ts, D_MODEL),
            st["ret_p"], st["hg_p"],
            st["wk0_p"], st["wv0_p"], st["wk1_p"], st["wv1_p"], st["wk2_p"], st["wv2_p"],
            st["mk_p"], st["mv_p"],
            st["ret_s"], st["hg_s"],
            st["wk0_s"], st["wv0_s"], st["wk1_s"], st["wv1_s"], st["wk2_s"], st["wv2_s"])
```

```python
import functools
import math

import numpy as np
import jax
import jax.numpy as jnp
from jax import lax
from jax.experimental import pallas as pl
from jax.experimental.pallas import tpu as pltpu

D_MODEL = 1024
DEPTH = 2
PAST_LEN = 2048
D_FF = 2816
BRANCH_W = 512
RET_HEADS = 4
RET_DK = 64
RET_DV = 128
RET_THETA = 10000.0
RET_CHUNK = 128
HG_HEADS = 4
HG_DK = 128
HG_DV = 128
HG_CHUNK = 64
DIL_GROUPS = ((128, 1), (512, 4), (2048, 16))
N_GROUPS = 3
DIL_HEADS = 8
DIL_DH = 64
ROPE_DIM = DIL_DH // 4
ROPE_THETA = 500000.0
DIL_QB = 128
MEM_LEN = 256
X_HEADS = 4
X_DH = D_MODEL // X_HEADS
IN_COLS = 11264
ALPHA = (2 * DEPTH) ** 0.25
EPS = 1e-5
NEG = -1e30
EXP_CLIP = 80.0
F32 = jnp.float32
BF16 = jnp.bfloat16

COL_AQK = 0
COL_AV = 1
COL_AG = 2
COL_BQ = 3
COL_BF = 4
COL_BI = 5
COL_BG = 6
COL_C0 = 7
MAIN_COLS = 10 * 512
W_IN_CQ, W_IN_CK, W_IN_CV, W_IN_GATE = 3584, 5120, 6656, 8192

VMEM_LIMIT = 56 * 1024 * 1024


def _cparams(*sem):
    return pltpu.CompilerParams(dimension_semantics=sem, vmem_limit_bytes=VMEM_LIMIT)


def _dot(a, b):
    return jnp.dot(a, b, preferred_element_type=F32)


def _dot_nt(a, b):
    return lax.dot_general(a, b, (((1,), (1,)), ((), ())), preferred_element_type=F32)


def _dot_tn(a, b):
    return lax.dot_general(a, b, (((0,), (0,)), ((), ())), preferred_element_type=F32)


def _ln(y, g, b):
    yc = y - jnp.mean(y, -1, keepdims=True)
    var = jnp.mean(yc * yc, -1, keepdims=True)
    return yc * lax.rsqrt(var + EPS) * g + b


def _silu(x):
    return x * jax.nn.sigmoid(x)


def _ffn_kernel(x_ref, wg_ref, wu_ref, wd_ref, g_ref, b_ref, o_ref, ob_ref, acc_ref, xb_ref):
    j = pl.program_id(1)

    @pl.when(j == 0)
    def _():
        acc_ref[...] = jnp.zeros_like(acc_ref)
        xb_ref[...] = x_ref[...].astype(BF16)

    xb = xb_ref[...]
    hg = _dot(xb, wg_ref[...])
    hu = _dot(xb, wu_ref[...])
    hid = (_silu(hg) * hu).astype(BF16)
    acc_ref[...] += _dot(hid, wd_ref[...])

    @pl.when(j == pl.num_programs(1) - 1)
    def _():
        y = _ln(ALPHA * x_ref[...] + 0.5 * acc_ref[...], g_ref[...], b_ref[...])
        o_ref[...] = y
        ob_ref[...] = y.astype(BF16)


def _ffn_ln(x, wg, wu, wd, g, b, tm=512, tf=D_FF // 2):
    n = x.shape[0]
    row = pl.BlockSpec((tm, D_MODEL), lambda i, j: (i, 0))
    return pl.pallas_call(
        _ffn_kernel,
        grid=(n // tm, D_FF // tf),
        in_specs=[
            row,
            pl.BlockSpec((D_MODEL, tf), lambda i, j: (0, j)),
            pl.BlockSpec((D_MODEL, tf), lambda i, j: (0, j)),
            pl.BlockSpec((tf, D_MODEL), lambda i, j: (j, 0)),
            pl.BlockSpec((1, D_MODEL), lambda i, j: (0, 0)),
            pl.BlockSpec((1, D_MODEL), lambda i, j: (0, 0)),
        ],
        out_specs=[row, row],
        out_shape=[jax.ShapeDtypeStruct((n, D_MODEL), F32), jax.ShapeDtypeStruct((n, D_MODEL), BF16)],
        scratch_shapes=[pltpu.VMEM((tm, D_MODEL), F32), pltpu.VMEM((tm, D_MODEL), BF16)],
        compiler_params=_cparams("parallel", "arbitrary"),
        name="ffn_ln",
    )(x, wg, wu, wd, g, b)


def _rot128(y, c, s_lo, s_hi, half):
    return y * c + pltpu.roll(y, 128 - half, 1) * s_lo + pltpu.roll(y, half, 1) * s_hi


def _inproj_kernel(x_ref, w_ref, tab_ref, o_ref, *, ret_blocks, dil_blocks):
    j = pl.program_id(1)
    o_ref[...] = _dot(x_ref[...], w_ref[...])

    def among(blocks):
        hit = j == blocks[0]
        for c in blocks[1:]:
            hit = jnp.logical_or(hit, j == c)
        return hit

    for blocks, t0, half in ((ret_blocks, 0, RET_DK // 2), (dil_blocks, 3, ROPE_DIM // 2)):
        if not blocks:
            continue

        @pl.when(among(blocks))
        def _(t0=t0, half=half):
            for c in range(4):
                sl = slice(c * 128, (c + 1) * 128)
                o_ref[:, sl] = _rot128(o_ref[:, sl], tab_ref[t0], tab_ref[t0 + 1], tab_ref[t0 + 2], half)


def _inproj(x, w, tabs, tm, ret_blocks, dil_blocks):
    n = x.shape[0]
    cols = w.shape[1]
    nt = tabs.shape[1] // tm
    return pl.pallas_call(
        functools.partial(_inproj_kernel, ret_blocks=ret_blocks, dil_blocks=dil_blocks),
        grid=(n // tm, cols // 512),
        in_specs=[
            pl.BlockSpec((tm, D_MODEL), lambda i, j: (i, 0)),
            pl.BlockSpec((D_MODEL, 512), lambda i, j: (0, j)),
            pl.BlockSpec((6, tm, 128), lambda i, j: (0, i % nt, 0)),
        ],
        out_specs=pl.BlockSpec((tm, 512), lambda i, j: (i, j)),
        out_shape=jax.ShapeDtypeStruct((n, cols), F32),
        compiler_params=_cparams("parallel", "arbitrary"),
        name="in_proj_rotary",
    )(x, w, tabs)


def _mm_kernel(x_ref, w_ref, o_ref):
    o_ref[...] = _dot(x_ref[...].astype(BF16), w_ref[...])


def _matmul(x, w, tm, tn=512):
    n, k = x.shape
    m = w.shape[1]
    return pl.pallas_call(
        _mm_kernel,
        grid=(n // tm, m // tn),
        in_specs=[pl.BlockSpec((tm, k), lambda i, j: (i, 0)),
                  pl.BlockSpec((k, tn), lambda i, j: (0, j))],
        out_specs=pl.BlockSpec((tm, tn), lambda i, j: (i, j)),
        out_shape=jax.ShapeDtypeStruct((n, m), F32),
        compiler_params=_cparams("parallel", "arbitrary"),
        name="matmul",
    )(x, w)


def _mm_res_ln_kernel(x_ref, w_ref, r_ref, g_ref, b_ref, o_ref):
    y = _dot(x_ref[...].astype(BF16), w_ref[...])
    o_ref[...] = _ln(ALPHA * r_ref[...] + y, g_ref[...], b_ref[...])


def _matmul_res_ln(x, w, res, g, b, tm):
    n = x.shape[0]
    return pl.pallas_call(
        _mm_res_ln_kernel,
        grid=(n // tm,),
        in_specs=[pl.BlockSpec((tm, D_MODEL), lambda i: (i, 0)),
                  pl.BlockSpec((D_MODEL, D_MODEL), lambda i: (0, 0)),
                  pl.BlockSpec((tm, D_MODEL), lambda i: (i, 0)),
                  pl.BlockSpec((1, D_MODEL), lambda i: (0, 0)),
                  pl.BlockSpec((1, D_MODEL), lambda i: (0, 0))],
        out_specs=pl.BlockSpec((tm, D_MODEL), lambda i: (i, 0)),
        out_shape=jax.ShapeDtypeStruct((n, D_MODEL), F32),
        compiler_params=_cparams("parallel"),
        name="proj_res_ln",
    )(x, w, res, g, b)


def _ret_gammas():
    return [1.0 - 2.0 ** (-5.0 - h) for h in range(RET_HEADS)]


def _ret_kernel(qk_ref, v_ref, s0_ref, din_ref, dq_ref, dk_ref, o_ref, so_ref, r_scr, *, nchunk, dc):
    c = RET_CHUNK

    @pl.when(pl.program_id(1) == 0)
    def _():
        r_scr[...] = s0_ref[0]

    def chunk(ci, carry):
        row = pl.multiple_of(ci * c, c)
        qk = qk_ref[0, pl.ds(row, c), :]
        v = v_ref[0, pl.ds(row, c), :]
        for h in range(RET_HEADS):
            q = qk[:, h * RET_DK:(h + 1) * RET_DK] * (RET_DK ** -0.5)
            k = qk[:, 256 + h * RET_DK:256 + (h + 1) * RET_DK]
            vh = v[:, h * RET_DV:(h + 1) * RET_DV].astype(BF16)
            r = r_scr[h]
            s = _dot_nt(q.astype(BF16), k.astype(BF16)) * din_ref[h]
            o = _dot(s.astype(BF16), vh) + _dot((q * dq_ref[h]).astype(BF16), r.astype(BF16))
            r_scr[h] = r * dc[h] + _dot_tn((k * dk_ref[h]).astype(BF16), vh)
            o_ref[0, pl.ds(row, c), h * RET_DV:(h + 1) * RET_DV] = o
        return carry

    lax.fori_loop(0, nchunk, chunk, 0)

    @pl.when(pl.program_id(1) == pl.num_programs(1) - 1)
    def _():
        so_ref[0] = r_scr[...]


def _retention_prompt(proj3, s0, tt=1024):
    b, t, _ = proj3.shape
    c = RET_CHUNK
    gam = np.array(_ret_gammas(), np.float64)
    i = np.arange(c, dtype=np.float64)
    diff = i[:, None] - i[None, :]
    din = np.where(diff >= 0, gam[:, None, None] ** np.maximum(diff, 0.0), 0.0)
    dq = np.broadcast_to((gam[:, None] ** (i + 1.0))[:, :, None], (RET_HEADS, c, RET_DK))
    dk = np.broadcast_to((gam[:, None] ** (c - 1.0 - i))[:, :, None], (RET_HEADS, c, RET_DK))
    dc = tuple(float(g ** c) for g in gam)
    const = lambda shape: pl.BlockSpec(shape, lambda bi, ti: (0,) * len(shape))
    return pl.pallas_call(
        functools.partial(_ret_kernel, nchunk=tt // c, dc=dc),
        grid=(b, t // tt),
        in_specs=[
            pl.BlockSpec((1, tt, 512), lambda bi, ti: (bi, ti, COL_AQK)),
            pl.BlockSpec((1, tt, 512), lambda bi, ti: (bi, ti, COL_AV)),
            pl.BlockSpec((1, RET_HEADS, RET_DK, RET_DV), lambda bi, ti: (bi, 0, 0, 0)),
            const((RET_HEADS, c, c)), const((RET_HEADS, c, RET_DK)), const((RET_HEADS, c, RET_DK)),
        ],
        out_specs=[
            pl.BlockSpec((1, tt, 512), lambda bi, ti: (bi, ti, 0)),
            pl.BlockSpec((1, RET_HEADS, RET_DK, RET_DV), lambda bi, ti: (bi, 0, 0, 0)),
        ],
        out_shape=[jax.ShapeDtypeStruct((b, t, BRANCH_W), F32),
                   jax.ShapeDtypeStruct((b, RET_HEADS, RET_DK, RET_DV), F32)],
        scratch_shapes=[pltpu.VMEM((RET_HEADS, RET_DK, RET_DV), F32)],
        compiler_params=_cparams("parallel", "arbitrary"),
        name="retention_chunked",
    )(proj3, proj3, s0, jnp.asarray(din, F32), jnp.asarray(dq, F32), jnp.asarray(dk, F32))


def _hgrn_log_f(z, lb):
    log_sig = jnp.minimum(z, 0.0) - jnp.log1p(jnp.exp(-jnp.abs(z)))
    return log_sig + jnp.log1p(lb * jnp.exp(jnp.minimum(-z, EXP_CLIP)))


def _cumsum_rows(g):
    n = g.shape[0]
    row = lax.broadcasted_iota(jnp.int32, g.shape, 0)
    sh = 1
    while sh < n:
        g = g + jnp.where(row >= sh, pltpu.roll(g, sh, 0), 0.0)
        sh *= 2
    return g


def _hgrn_kernel(q_ref, f_ref, i_ref, lb_ref, s0_ref, o_ref, so_ref, s_scr, *, nchunk):
    c = HG_CHUNK
    nslab = c // 8

    @pl.when(pl.program_id(2) == 0)
    def _():
        s_scr[...] = s0_ref[0, 0]

    lb = lb_ref[...]
    sub = 16
    rowi = lax.broadcasted_iota(jnp.int32, (8, HG_DK), 0)
    rowc = lax.broadcasted_iota(jnp.int32, (c, HG_DK), 0)

    def chunk(ci, carry):
        row = pl.multiple_of(ci * c, c)
        qh = _silu(q_ref[0, pl.ds(row, c), :]) * (HG_DK ** -0.5)
        g = _hgrn_log_f(f_ref[0, pl.ds(row, c), :], lb)
        v = i_ref[0, pl.ds(row, c), :]
        b = _cumsum_rows(g)
        ki = 1.0 - jnp.exp(g)
        s_mat = s_scr[...]
        vb = v.astype(BF16)
        o = _dot((qh * jnp.exp(b)).astype(BF16), s_mat.astype(BF16))
        a_blocks = [jnp.zeros((sub, c), F32)]
        for blk in range(1, c // sub):
            lo = blk * sub
            r = b[lo - 1:lo]
            qt = qh[lo:lo + sub] * jnp.exp(b[lo:lo + sub] - r)
            kt = jnp.where(rowc < lo, ki * jnp.exp(jnp.minimum(r - b, 0.0)), 0.0)
            a_blocks.append(_dot_nt(qt.astype(BF16), kt.astype(BF16)))
        o = o + _dot(jnp.concatenate(a_blocks, axis=0).astype(BF16), vb)
        o_sl = [o[8 * j:8 * j + 8] for j in range(nslab)]
        b_sl = [b[8 * j:8 * j + 8] for j in range(nslab)]
        q_sl = [qh[8 * j:8 * j + 8] for j in range(nslab)]
        for s in range(c):
            js = s // 8
            bs, ks, vs = b[s:s + 1], ki[s:s + 1], v[s:s + 1]
            for j in range(js, (s // sub + 1) * (sub // 8)):
                d = b_sl[j] - bs
                if j == js:
                    d = jnp.where(rowi >= s % 8, d, NEG)
                a = jnp.sum(q_sl[j] * ks * jnp.exp(d), axis=-1, keepdims=True)
                o_sl[j] = o_sl[j] + a * vs
        o_ref[0, pl.ds(row, c), :] = jnp.concatenate(o_sl, axis=0)
        b_end = b[c - 1:c]
        decay_col = jnp.broadcast_to(jnp.exp(b_end), (HG_DK, HG_DK)).T
        s_scr[...] = s_mat * decay_col + _dot_tn((ki * jnp.exp(b_end - b)).astype(BF16), v.astype(BF16))
        return carry

    lax.fori_loop(0, nchunk, chunk, 0)

    @pl.when(pl.program_id(2) == pl.num_programs(2) - 1)
    def _():
        so_ref[0, 0] = s_scr[...]


def _hgrn_prompt(proj3, lb, s0, tt=1024):
    b, t, _ = proj3.shape
    col = lambda base: (lambda bi, h, ti: (bi, ti, base * 4 + h))
    return pl.pallas_call(
        functools.partial(_hgrn_kernel, nchunk=tt // HG_CHUNK),
        grid=(b, HG_HEADS, t // tt),
        in_specs=[
            pl.BlockSpec((1, tt, 128), col(COL_BQ)),
            pl.BlockSpec((1, tt, 128), col(COL_BF)),
            pl.BlockSpec((1, tt, 128), col(COL_BI)),
            pl.BlockSpec((1, 128), lambda bi, h, ti: (0, h)),
            pl.BlockSpec((1, 1, HG_DK, HG_DV), lambda bi, h, ti: (bi, h, 0, 0)),
        ],
        out_specs=[
            pl.BlockSpec((1, tt, 128), lambda bi, h, ti: (bi, ti, h)),
            pl.BlockSpec((1, 1, HG_DK, HG_DV), lambda bi, h, ti: (bi, h, 0, 0)),
        ],
        out_shape=[jax.ShapeDtypeStruct((b, t, BRANCH_W), F32),
                   jax.ShapeDtypeStruct((b, HG_HEADS, HG_DK, HG_DV), F32)],
        scratch_shapes=[pltpu.VMEM((HG_DK, HG_DV), F32)],
        compiler_params=_cparams("parallel", "parallel", "arbitrary"),
        name="hgrn2_chunked",
    )(proj3, proj3, proj3, lb, s0)


def _dil_kernel(q_ref, kc_ref, kp_ref, vc_ref, vp_ref, o_ref, lse_ref, s_scr, p_scr):
    i = pl.program_id(2)
    qb = DIL_QB
    npair = DIL_HEADS // 2
    pairs = [slice(c * 128, (c + 1) * 128) for c in range(npair)]
    first = lax.broadcasted_iota(jnp.int32, (qb, 128), 1) < DIL_DH
    for c, psl in enumerate(pairs):
        q = q_ref[:, psl] * (DIL_DH ** -0.5)
        kp = kp_ref[:, psl].astype(BF16)
        kc = kc_ref[:, psl].astype(BF16)
        for half in range(2):
            qm = jnp.where(first if half == 0 else jnp.logical_not(first), q, 0.0).astype(BF16)
            s_scr[2 * c + half, :, 0:qb] = _dot_nt(qm, kp)
            s_scr[2 * c + half, :, qb:2 * qb] = _dot_nt(qm, kc)
    t = lax.broadcasted_iota(jnp.int32, (qb, qb), 0)
    s = lax.broadcasted_iota(jnp.int32, (qb, qb), 1)
    in_prev = jnp.logical_and(s >= t, i > 0)
    in_cur = s <= t
    for c, psl in enumerate(pairs):
        lses = []
        for half in range(2):
            h = 2 * c + half
            sp = jnp.where(in_prev, s_scr[h, :, 0:qb], NEG)
            sc = jnp.where(in_cur, s_scr[h, :, qb:2 * qb], NEG)
            m = jnp.max(jnp.maximum(sp, sc), -1, keepdims=True)
            pp = jnp.exp(sp - m)
            pc = jnp.exp(sc - m)
            l = jnp.sum(pp + pc, -1, keepdims=True)
            p_scr[h, :, 0:qb] = (pp / l).astype(BF16)
            p_scr[h, :, qb:2 * qb] = (pc / l).astype(BF16)
            lses.append(m + jnp.log(l))
        lse_ref[:, psl] = jnp.where(first, lses[0], lses[1])
    for c, psl in enumerate(pairs):
        vp = vp_ref[:, psl].astype(BF16)
        vc = vc_ref[:, psl].astype(BF16)
        o2 = [_dot(p_scr[2 * c + half, :, 0:qb], vp) + _dot(p_scr[2 * c + half, :, qb:2 * qb], vc)
              for half in range(2)]
        o_ref[:, psl] = jnp.where(first, o2[0], o2[1])


def _dilated_prompt(qkv4, col0, g):
    b, dil, tr, _ = qkv4.shape
    assert dil == DIL_GROUPS[g][1] and tr % DIL_QB == 0
    cur = lambda c: (lambda bi, r, i: (bi, r, i, c))
    prev = lambda c: (lambda bi, r, i: (bi, r, jnp.maximum(i - 1, 0), c))
    blk = (None, None, DIL_QB, 512)
    oblk = (None, DIL_QB, 512)
    o, lse = pl.pallas_call(
        _dil_kernel,
        grid=(b, dil, tr // DIL_QB),
        in_specs=[pl.BlockSpec(blk, cur(col0)),
                  pl.BlockSpec(blk, cur(col0 + 1)), pl.BlockSpec(blk, prev(col0 + 1)),
                  pl.BlockSpec(blk, cur(col0 + 2)), pl.BlockSpec(blk, prev(col0 + 2))],
        out_specs=[pl.BlockSpec(oblk, lambda bi, r, i: (bi, i, r)),
                   pl.BlockSpec(oblk, lambda bi, r, i: (bi, i, r))],
        out_shape=[jax.ShapeDtypeStruct((b, tr, dil * 512), F32),
                   jax.ShapeDtypeStruct((b, tr, dil * 512), F32)],
        scratch_shapes=[pltpu.VMEM((DIL_HEADS, DIL_QB, 2 * DIL_QB), F32),
                        pltpu.VMEM((DIL_HEADS, DIL_QB, 2 * DIL_QB), BF16)],
        compiler_params=_cparams("parallel", "parallel", "arbitrary"),
        name="dilated_attn_%d" % g,
    )(qkv4, qkv4, qkv4, qkv4, qkv4)
    return o.reshape(b * tr * dil, 512), lse.reshape(b * tr * dil, 512)


def _head_norm(x, centre):
    parts = []
    for h in range(4):
        xh = x[:, h * 128:(h + 1) * 128]
        if centre:
            xh = xh - jnp.mean(xh, -1, keepdims=True)
        parts.append(xh * lax.rsqrt(jnp.mean(xh * xh, -1, keepdims=True) + EPS))
    return jnp.concatenate(parts, axis=1)


def _merge_kernel(*refs, n_groups):
    h_ref, hb_ref, oa_ref, ag_ref, ob_ref, bg_ref = refs[:6]
    c_refs = refs[6:6 + (1 if n_groups == 1 else 2 * n_groups)]
    wgate_ref, gn_ref, hn_ref, wb_ref, wo_ref, lg_ref, lb_ref, o_ref = refs[6 + len(c_refs):]
    if n_groups == 1:
        oc = c_refs[0][...]
    else:
        ls = [r[...] for r in c_refs[n_groups:]]
        m = functools.reduce(jnp.maximum, ls)
        es = [jnp.exp(x - m) for x in ls]
        den = functools.reduce(lambda a, b: a + b, es)
        oc = functools.reduce(lambda a, b: a + b, [(e / den) * r[...] for e, r in zip(es, c_refs[:n_groups])])
    oa = _head_norm(oa_ref[...], True) * gn_ref[...] * _silu(ag_ref[...])
    ob = _head_norm(ob_ref[...], False) * hn_ref[...] * jax.nn.sigmoid(bg_ref[...])
    hb = hb_ref[...]
    merged = None
    for br, x in enumerate((oa, ob, oc)):
        gate = jax.nn.sigmoid(_dot(hb, wgate_ref[:, br * D_MODEL:(br + 1) * D_MODEL]))
        term = gate * _dot(x.astype(BF16), wb_ref[br])
        merged = term if merged is None else merged + term
    mix = _dot(merged.astype(BF16), wo_ref[...])
    o_ref[...] = _ln(ALPHA * h_ref[...] + mix, lg_ref[...], lb_ref[...])


def _merge(h, hb, proj, oa, ob, c_parts, wgate, gn, hn, wb, wo, lg, lb, tm=256):
    n = h.shape[0]
    n_groups = 1 if len(c_parts) == 1 else len(c_parts) // 2
    row512 = lambda cb: pl.BlockSpec((tm, 512), lambda i: (i, cb))
    row1024 = pl.BlockSpec((tm, D_MODEL), lambda i: (i, 0))
    const = lambda shape: pl.BlockSpec(shape, lambda i: (0,) * len(shape))
    return pl.pallas_call(
        functools.partial(_merge_kernel, n_groups=n_groups),
        grid=(n // tm,),
        in_specs=[row1024, row1024, row512(0), row512(COL_AG), row512(0), row512(COL_BG)]
                 + [row512(0)] * len(c_parts)
                 + [const((D_MODEL, 3 * D_MODEL)), const((1, 512)), const((1, 512)),
                    const((3, 512, D_MODEL)), const((D_MODEL, D_MODEL)),
                    const((1, D_MODEL)), const((1, D_MODEL))],
        out_specs=row1024,
        out_shape=jax.ShapeDtypeStruct((n, D_MODEL), F32),
        compiler_params=_cparams("parallel"),
        name="branch_merge",
    )(h, hb, oa, proj, ob, proj, *c_parts, wgate, gn, hn, wb, wo, lg, lb)


def _xattn_block_kernel(h_ref, wq_ref, k_ref, v_ref, wo_ref, g_ref, b_ref, o_ref):
    h = h_ref[0]
    q = _dot(h.astype(BF16), wq_ref[...])
    outs = []
    for hd in range(X_HEADS):
        sl = slice(hd * X_DH, (hd + 1) * X_DH)
        s = _dot_nt(q[:, sl].astype(BF16), k_ref[0, :, sl]) * (X_DH ** -0.5)
        m = jnp.max(s, -1, keepdims=True)
        p = jnp.exp(s - m)
        p = p / jnp.sum(p, -1, keepdims=True)
        outs.append(_dot(p.astype(BF16), v_ref[0, :, sl]).astype(BF16))
    y = _dot(jnp.concatenate(outs, axis=1), wo_ref[...])
    o_ref[0] = _ln(ALPHA * h + y, g_ref[...], b_ref[...])


def _xattn_block(h3, wq, mk, mv, wo, g, b, tq=512):
    bsz, t, _ = h3.shape
    row = pl.BlockSpec((1, tq, D_MODEL), lambda bi, ti: (bi, ti, 0))
    mem = pl.BlockSpec((1, MEM_LEN, D_MODEL), lambda bi, ti: (bi, 0, 0))
    wspec = pl.BlockSpec((D_MODEL, D_MODEL), lambda bi, ti: (0, 0))
    vec = pl.BlockSpec((1, D_MODEL), lambda bi, ti: (0, 0))
    return pl.pallas_call(
        _xattn_block_kernel,
        grid=(bsz, t // tq),
        in_specs=[row, wspec, mem, mem, wspec, vec, vec],
        out_specs=row,
        out_shape=jax.ShapeDtypeStruct((bsz, t, D_MODEL), F32),
        compiler_params=_cparams("parallel", "arbitrary"),
        name="cross_attn_block",
    )(h3, wq, mk, mv, wo, g, b)


def _xattn_s_kernel(q_ref, k_ref, v_ref, o_ref, *, bb):
    nq = q_ref.shape[1]
    nk = MEM_LEN * X_HEADS
    same_head = (lax.broadcasted_iota(jnp.int32, (nq, nk), 0) % X_HEADS
                 == lax.broadcasted_iota(jnp.int32, (nq, nk), 1) % X_HEADS)
    for bi in range(bb):
        k2 = k_ref[bi].reshape(nk, X_DH).astype(BF16)
        v2 = v_ref[bi].reshape(nk, X_DH).astype(BF16)
        s = _dot_nt(q_ref[bi].astype(BF16), k2) * (X_DH ** -0.5)
        s = jnp.where(same_head, s, NEG)
        m = jnp.max(s, -1, keepdims=True)
        p = jnp.exp(s - m)
        p = p / jnp.sum(p, -1, keepdims=True)
        o_ref[bi] = _dot(p.astype(BF16), v2)


def _xattn_sample(q3, mk5, mv5, layer, bb=2):
    b, nq, _ = q3.shape
    mem_spec = pl.BlockSpec((None, bb, MEM_LEN, X_HEADS, X_DH), lambda bi: (layer, bi, 0, 0, 0))
    row = pl.BlockSpec((bb, nq, X_DH), lambda bi: (bi, 0, 0))
    return pl.pallas_call(
        functools.partial(_xattn_s_kernel, bb=bb),
        grid=(b // bb,),
        in_specs=[row, mem_spec, mem_spec],
        out_specs=row,
        out_shape=jax.ShapeDtypeStruct((b, nq, X_DH), F32),
        compiler_params=_cparams("parallel"),
        name="cross_attn_step",
    )(q3, mk5, mv5)


def _pick_col(tile, onehot):
    return jnp.sum(jnp.where(onehot, tile, 0.0), axis=1, keepdims=True)


def _pick_row(slab, sub_hit):
    return jnp.sum(jnp.where(sub_hit, slab, 0.0), axis=0, keepdims=True)


def _ret_s_kernel(q_ref, k_ref, v_ref, s_ref, o_ref, so_ref, qt_scr, kt_scr, *, nt, bb):
    j = pl.program_id(0)
    nb = q_ref.shape[1]

    @pl.when(j == 0)
    def _():
        for t in range(nt):
            for c in range(2):
                sl = slice(c * 128, (c + 1) * 128)
                qt_scr[t, sl, :] = (q_ref[t, :, sl] * (RET_DK ** -0.5)).T
                kt_scr[t, sl, :] = k_ref[t, :, sl].T

    lane = lax.broadcasted_iota(jnp.int32, (RET_DK, nb), 1)
    sub = lax.broadcasted_iota(jnp.int32, (8, 128), 0)
    o_ref[...] = jnp.zeros_like(o_ref)
    gam = _ret_gammas()

    def body(bi, carry):
        bg = j * bb + bi
        onehot = lane == bg
        row_g = pl.multiple_of((bg // 8) * 8, 8)
        row_l = pl.multiple_of((bi // 8) * 8, 8)
        sub_hit = sub == bi % 8
        for h in range(RET_HEADS):
            r = s_ref[bi, h]
            for t in range(nt):
                ksl = slice(h * RET_DK, (h + 1) * RET_DK)
                qc = _pick_col(qt_scr[t, ksl, :], onehot)
                kc = _pick_col(kt_scr[t, ksl, :], onehot)
                vsl = slice(h * RET_DV, (h + 1) * RET_DV)
                vrow = _pick_row(v_ref[t, pl.ds(row_g, 8), vsl], sub_hit)
                r = r * gam[h] + kc * vrow
                orow = jnp.sum(r * qc, axis=0, keepdims=True)
                o_ref[t, pl.ds(row_l, 8), vsl] = jnp.where(sub_hit, orow, o_ref[t, pl.ds(row_l, 8), vsl])
            so_ref[bi, h] = r
        return carry

    lax.fori_loop(0, bb, body, 0)


def _retention_sample(q_t, k_t, v_t, state, layer, bb=16):
    nt, nb, _ = q_t.shape
    full = lambda w: pl.BlockSpec((nt, nb, w), lambda j: (0, 0, 0))
    return pl.pallas_call(
        functools.partial(_ret_s_kernel, nt=nt, bb=bb),
        grid=(nb // bb,),
        in_specs=[full(256), full(256), full(512),
                  pl.BlockSpec((None, bb, RET_HEADS, RET_DK, RET_DV), lambda j: (layer, j, 0, 0, 0))],
        out_specs=[pl.BlockSpec((nt, bb, 512), lambda j: (0, j, 0)),
                   pl.BlockSpec((bb, RET_HEADS, RET_DK, RET_DV), lambda j: (j, 0, 0, 0))],
        out_shape=[jax.ShapeDtypeStruct((nt, nb, 512), F32),
                   jax.ShapeDtypeStruct((nb, RET_HEADS, RET_DK, RET_DV), F32)],
        scratch_shapes=[pltpu.VMEM((nt, 256, nb), F32), pltpu.VMEM((nt, 256, nb), F32)],
        compiler_params=_cparams("arbitrary"),
        name="retention_step",
    )(q_t, k_t, v_t, state)


def _hgrn_s_kernel(q_ref, f_ref, i_ref, lb_ref, s_ref, o_ref, so_ref, qt_scr, ft_scr, kt_scr, *, nt, bb):
    j = pl.program_id(0)
    nb = q_ref.shape[1]

    @pl.when(j == 0)
    def _():
        for t in range(nt):
            for h in range(HG_HEADS):
                sl = slice(h * HG_DK, (h + 1) * HG_DK)
                f = jnp.exp(_hgrn_log_f(f_ref[t, :, sl], lb_ref[:, sl]))
                qt_scr[t, sl, :] = (_silu(q_ref[t, :, sl]) * (HG_DK ** -0.5)).T
                ft_scr[t, sl, :] = f.T
                kt_scr[t, sl, :] = (1.0 - f).T

    lane = lax.broadcasted_iota(jnp.int32, (HG_DK, nb), 1)
    sub = lax.broadcasted_iota(jnp.int32, (8, 128), 0)
    o_ref[...] = jnp.zeros_like(o_ref)

    def body(bi, carry):
        bg = j * bb + bi
        onehot = lane == bg
        row_g = pl.multiple_of((bg // 8) * 8, 8)
        row_l = pl.multiple_of((bi // 8) * 8, 8)
        sub_hit = sub == bi % 8
        for h in range(HG_HEADS):
            sl = slice(h * HG_DK, (h + 1) * HG_DK)
            s_mat = s_ref[bi, h]
            for t in range(nt):
                qc = _pick_col(qt_scr[t, sl, :], onehot)
                fc = _pick_col(ft_scr[t, sl, :], onehot)
                kc = _pick_col(kt_scr[t, sl, :], onehot)
                vrow = _pick_row(i_ref[t, pl.ds(row_g, 8), sl], sub_hit)
                s_mat = s_mat * fc + kc * vrow
                orow = jnp.sum(s_mat * qc, axis=0, keepdims=True)
                o_ref[t, pl.ds(row_l, 8), sl] = jnp.where(sub_hit, orow, o_ref[t, pl.ds(row_l, 8), sl])
            so_ref[bi, h] = s_mat
        return carry

    lax.fori_loop(0, bb, body, 0)


def _hgrn_sample(q_t, f_t, i_t, lb, state, layer, bb=16):
    nt, nb, _ = q_t.shape
    full = pl.BlockSpec((nt, nb, 512), lambda j: (0, 0, 0))
    return pl.pallas_call(
        functools.partial(_hgrn_s_kernel, nt=nt, bb=bb),
        grid=(nb // bb,),
        in_specs=[full, full, full, pl.BlockSpec((1, 512), lambda j: (0, 0)),
                  pl.BlockSpec((None, bb, HG_HEADS, HG_DK, HG_DV), lambda j: (layer, j, 0, 0, 0))],
        out_specs=[pl.BlockSpec((nt, bb, 512), lambda j: (0, j, 0)),
                   pl.BlockSpec((bb, HG_HEADS, HG_DK, HG_DV), lambda j: (j, 0, 0, 0))],
        out_shape=[jax.ShapeDtypeStruct((nt, nb, 512), F32),
                   jax.ShapeDtypeStruct((nb, HG_HEADS, HG_DK, HG_DV), F32)],
        scratch_shapes=[pltpu.VMEM((nt, 512, nb), F32)] * 3,
        compiler_params=_cparams("arbitrary"),
        name="hgrn2_step",
    )(q_t, f_t, i_t, lb, state)


def _dil_s_kernel(new_ref, k0_ref, v0_ref, k1_ref, v1_ref, k2_ref, v2_ref, o_ref, qpad_scr, opad_scr, *, nt):
    caches = ((k0_ref, v0_ref), (k1_ref, v1_ref), (k2_ref, v2_ref))
    scale = DIL_DH ** -0.5
    lane = lax.broadcasted_iota(jnp.int32, (DIL_DH, 128), 1)
    sub8 = lax.broadcasted_iota(jnp.int32, (DIL_HEADS, 128), 0)
    lane8 = lax.broadcasted_iota(jnp.int32, (DIL_HEADS, 128), 1)
    col_id = lambda g, i, h: (g * nt + i) * DIL_HEADS + h

    qpad_scr[...] = jnp.zeros_like(qpad_scr)
    opad_scr[...] = jnp.zeros_like(opad_scr)
    for g in range(N_GROUPS):
        for i in range(nt):
            r0 = col_id(g, i, 0)
            qpad_scr[r0:r0 + DIL_HEADS, 0:DIL_DH] = new_ref[0, i, g]
    qt = qpad_scr[...].T[0:DIL_DH]
    qcol = lambda g, i, h: qt[:, col_id(g, i, h):col_id(g, i, h) + 1]

    oc = jnp.zeros((DIL_DH, 128), F32)
    m_all = [[None] * nt for _ in range(N_GROUPS)]
    l_all = [[None] * nt for _ in range(N_GROUPS)]
    pn_all = [[None] * nt for _ in range(N_GROUPS)]
    for g in range(N_GROUPS):
        kc_ref, vc_ref = caches[g]
        dil = DIL_GROUPS[g][1]
        ntile = kc_ref.shape[-1] // 128
        q_t = [new_ref[0, i, g] for i in range(nt)]
        kn = [new_ref[0, j, 3 + g] for j in range(nt)]
        new_score = lambda i, j: jnp.sum(q_t[i] * kn[j], axis=-1, keepdims=True) * scale
        if dil == 1:
            for i in range(nt):
                s = jnp.zeros((DIL_HEADS, 128), F32)
                for h in range(DIL_HEADS):
                    row = jnp.sum(kc_ref[0, h] * qcol(g, i, h), axis=0, keepdims=True)
                    s = jnp.where(sub8 == h, row, s)
                s = jnp.where(lane8 >= i, s * scale, NEG)
                sn = [new_score(i, j) for j in range(i + 1)]
                m = jnp.max(s, -1, keepdims=True)
                for x in sn:
                    m = jnp.maximum(m, x)
                p = jnp.exp(s - m)
                pn = [jnp.exp(x - m) for x in sn]
                l = jnp.sum(p, -1, keepdims=True)
                for x in pn:
                    l = l + x
                for h in range(DIL_HEADS):
                    col = jnp.sum(vc_ref[0, h] * p[h:h + 1, :], axis=1, keepdims=True)
                    oc = jnp.where(lane == col_id(g, i, h), col, oc)
                m_all[g][i], l_all[g][i], pn_all[g][i] = m, l, list(zip(pn, range(i + 1)))
        else:
            assert nt <= dil and 128 % dil == 0
            cls8 = [(lane8 & (dil - 1)) == i for i in range(nt)]
            cls = [(lane & (dil - 1)) == i for i in range(nt)]
            s_t = [jnp.zeros((DIL_HEADS, 128), F32) for _ in range(ntile)]
            for h in range(DIL_HEADS):
                qsel = jnp.zeros((DIL_DH, 128), F32)
                for i in range(nt):
                    qsel = jnp.where(cls[i], qcol(g, i, h), qsel)
                for t in range(ntile):
                    row = jnp.sum(kc_ref[0, h, :, t * 128:(t + 1) * 128] * qsel, axis=0, keepdims=True)
                    s_t[t] = jnp.where(sub8 == h, row, s_t[t])
            s_t = [s * scale for s in s_t]
            smax = s_t[0]
            for t in range(1, ntile):
                smax = jnp.maximum(smax, s_t[t])
            m_tile = jnp.zeros((DIL_HEADS, 128), F32)
            owned = cls8[0]
            for i in range(nt):
                sn = new_score(i, i)
                m = jnp.maximum(jnp.max(jnp.where(cls8[i], smax, NEG), -1, keepdims=True), sn)
                m_all[g][i] = m
                pn_all[g][i] = [(jnp.exp(sn - m), i)]
                m_tile = jnp.where(cls8[i], m, m_tile)
                owned = jnp.logical_or(owned, cls8[i])
            p_t = [jnp.exp(jnp.where(owned, s - m_tile, NEG)) for s in s_t]
            psum = p_t[0]
            for t in range(1, ntile):
                psum = psum + p_t[t]
            for i in range(nt):
                l_all[g][i] = jnp.sum(jnp.where(cls8[i], psum, 0.0), -1, keepdims=True) + pn_all[g][i][0][0]
            for h in range(DIL_HEADS):
                acc = jnp.zeros((DIL_DH, 128), F32)
                for t in range(ntile):
                    acc = acc + vc_ref[0, h, :, t * 128:(t + 1) * 128] * p_t[t][h:h + 1, :]
                for i in range(nt):
                    col = jnp.sum(jnp.where(cls[i], acc, 0.0), axis=1, keepdims=True)
                    oc = jnp.where(lane == col_id(g, i, h), col, oc)

    opad_scr[0:DIL_DH, :] = oc
    ot = opad_scr[...].T
    for i in range(nt):
        outs, lses = [], []
        for g in range(N_GROUPS):
            r0 = col_id(g, i, 0)
            o = ot[r0:r0 + DIL_HEADS, 0:DIL_DH]
            for pj, j in pn_all[g][i]:
                o = o + pj * new_ref[0, j, 6 + g]
            outs.append(o / l_all[g][i])
            lses.append(m_all[g][i] + jnp.log(l_all[g][i]))
        m = jnp.maximum(jnp.maximum(lses[0], lses[1]), lses[2])
        es = [jnp.exp(x - m) for x in lses]
        den = es[0] + es[1] + es[2]
        o_ref[0, i] = (es[0] / den) * outs[0] + (es[1] / den) * outs[1] + (es[2] / den) * outs[2]


def _dilated_sample(new9, caches, layer):
    nb, nt = new9.shape[:2]
    views, specs = [], []
    for g, (kbuf, vbuf) in enumerate(caches):
        w = kbuf.shape[2]
        assert w == DIL_GROUPS[g][0] and w // DIL_GROUPS[g][1] == 128
        for buf in (kbuf, vbuf):
            views.append(jnp.transpose(buf, (0, 1, 3, 4, 2)))
            specs.append(pl.BlockSpec((None, 1, DIL_HEADS, DIL_DH, w), lambda j: (layer, j, 0, 0, 0)))
    return pl.pallas_call(
        functools.partial(_dil_s_kernel, nt=nt),
        grid=(nb,),
        in_specs=[pl.BlockSpec((1, nt, 9, DIL_HEADS, DIL_DH), lambda j: (j, 0, 0, 0, 0))] + specs,
        out_specs=pl.BlockSpec((1, nt, DIL_HEADS, DIL_DH), lambda j: (j, 0, 0, 0)),
        out_shape=jax.ShapeDtypeStruct((nb, nt, DIL_HEADS, DIL_DH), F32),
        scratch_shapes=[pltpu.VMEM((128, 128), F32), pltpu.VMEM((128, 128), F32)],
        compiler_params=_cparams("parallel"),
        name="dilated_attn_step",
    )(new9, *views)


def _rotary_tables(pos):
    pos = jnp.asarray(np.asarray(pos), jnp.int32)
    lane = np.arange(128) % 64
    out = []
    for rot_dim, theta in ((RET_DK, RET_THETA), (ROPE_DIM, ROPE_THETA)):
        half = rot_dim // 2
        inv = jnp.power(jnp.float32(theta), -jnp.arange(half, dtype=F32) / half)
        ang = pos.astype(F32)[:, None] * inv[None, :]
        idx = lane % half
        cos = jnp.where(lane < rot_dim, jnp.cos(ang)[:, idx], 1.0)
        sin = jnp.sin(ang)[:, idx]
        out += [cos, jnp.where(lane < half, -sin, 0.0),
                jnp.where((lane >= half) & (lane < rot_dim), sin, 0.0)]
    return jnp.stack(out, 0).astype(F32)


def _hgrn_lower_bounds(raw):
    p = jax.nn.softmax(raw.astype(F32), axis=0)
    return jnp.cumsum(p, axis=0) - p[0:1]


def _row(v):
    return v.reshape(1, -1)


def kernel(x_prompt, x_sample, mem_prompt, state_ret, state_hgrn, cache_win_k0, cache_win_v0, cache_win_k1, cache_win_v1, cache_win_k2, cache_win_v2, cache_mem_k, cache_mem_v, ln_g, ln_b, ffn_w_gate, ffn_w_up, ffn_w_down, w_in, ret_gn_g, hgrn_lb_raw, hgrn_norm_g, w_branch, w_out, xattn_w_q, xattn_w_k, xattn_w_v, xattn_w_o):
    bp, tp, _ = x_prompt.shape
    bs, ts, _ = x_sample.shape
    np_, ns = bp * tp, bs * ts
    lb_all = _hgrn_lower_bounds(hgrn_lb_raw)
    tab_p = _rotary_tables(np.arange(tp))
    tab_res = [tab_p] + [_rotary_tables(np.arange(tp).reshape(tp // dil, dil).T.reshape(-1))
                         for _, dil in DIL_GROUPS[1:]]
    tm_s = ns
    tab_s = _rotary_tables(PAST_LEN + (np.arange(tm_s) % ts))
    caches = ((cache_win_k0, cache_win_v0), (cache_win_k1, cache_win_v1), (cache_win_k2, cache_win_v2))
    mem2 = mem_prompt.reshape(bp * MEM_LEN, D_MODEL)

    xp = x_prompt.reshape(np_, D_MODEL)
    xs = x_sample.reshape(ns, D_MODEL)
    tm_p = 1024
    acc = {k: [] for k in ("ret_p", "hg_p", "mk_p", "mv_p", "ret_s", "hg_s")}
    for g in range(N_GROUPS):
        for k in ("wk%d_p", "wv%d_p", "wk%d_s", "wv%d_s"):
            acc[k % g] = []

    for l in range(DEPTH):
        wg = ffn_w_gate[l].astype(BF16)
        wu = ffn_w_up[l].astype(BF16)
        wd = ffn_w_down[l].astype(BF16)
        win = w_in[l].astype(BF16)
        w_main = jnp.concatenate(
            [win[:, :W_IN_CQ], win[:, W_IN_CQ:W_IN_CQ + 512], win[:, W_IN_CK:W_IN_CK + 512],
             win[:, W_IN_CV:W_IN_CV + 512]], axis=1)
        w_gate = win[:, W_IN_GATE:]
        w_grp = [None] + [jnp.concatenate([win[:, o + g * 512:o + (g + 1) * 512]
                                           for o in (W_IN_CQ, W_IN_CK, W_IN_CV)], axis=1)
                          for g in range(1, N_GROUPS)]
        wb = w_branch[l].astype(BF16)
        wo = w_out[l].astype(BF16)
        wq = xattn_w_q[l].astype(BF16)
        wk = xattn_w_k[l].astype(BF16)
        wv = xattn_w_v[l].astype(BF16)
        wxo = xattn_w_o[l].astype(BF16)
        lng = [_row(ln_g[l, i]) for i in range(4)]
        lnb = [_row(ln_b[l, i]) for i in range(4)]
        gn, hn, lb = _row(ret_gn_g[l]), _row(hgrn_norm_g[l]), _row(lb_all[l])

        mk = _matmul(mem2, wk, tm=min(512, bp * MEM_LEN))
        mv = _matmul(mem2, wv, tm=min(512, bp * MEM_LEN))
        h, hb = _ffn_ln(xp, wg[0], wu[0], wd[0], lng[0], lnb[0])
        proj = _inproj(hb, w_main, tab_p, tm_p, (COL_AQK,), (COL_C0, COL_C0 + 1))
        proj3 = proj.reshape(bp, tp, MAIN_COLS)
        oa, rp = _retention_prompt(proj3, jnp.zeros((bp, RET_HEADS, RET_DK, RET_DV), F32))
        ob, gp = _hgrn_prompt(proj3, lb, jnp.zeros((bp, HG_HEADS, HG_DK, HG_DV), F32))
        dres = [_dilated_prompt(proj.reshape(bp, 1, tp, MAIN_COLS), COL_C0, 0)]
        pgrp = [None]
        for g in range(1, N_GROUPS):
            dil = DIL_GROUPS[g][1]
            h_res = hb.reshape(bp, tp // dil, dil, D_MODEL).transpose(0, 2, 1, 3).reshape(np_, D_MODEL)
            pg = _inproj(h_res, w_grp[g], tab_res[g], tm_p, (), (0, 1)).reshape(bp, dil, tp // dil, 1536)
            pgrp.append(pg)
            dres.append(_dilated_prompt(pg, 0, g))
        h = _merge(h, hb, proj, oa.reshape(np_, 512), ob.reshape(np_, 512),
                   [d[0] for d in dres] + [d[1] for d in dres], w_gate, gn, hn, wb, wo, lng[1], lnb[1])
        h = _xattn_block(h.reshape(bp, tp, D_MODEL), wq, mk.astype(BF16).reshape(bp, MEM_LEN, D_MODEL),
                         mv.astype(BF16).reshape(bp, MEM_LEN, D_MODEL), wxo, lng[2], lnb[2])
        xp, _ = _ffn_ln(h.reshape(np_, D_MODEL), wg[1], wu[1], wd[1], lng[3], lnb[3])

        acc["ret_p"].append(rp)
        acc["hg_p"].append(gp)
        acc["mk_p"].append(mk.reshape(bp, MEM_LEN, X_HEADS, X_DH))
        acc["mv_p"].append(mv.reshape(bp, MEM_LEN, X_HEADS, X_DH))
        for g, (window, dil) in enumerate(DIL_GROUPS):
            keep = min(window, tp)
            assert keep % dil == 0
            for name, off in (("wk%d_p", 1), ("wv%d_p", 2)):
                if g == 0:
                    c0 = (COL_C0 + off) * 512
                    rows = proj3[:, tp - keep:, c0:c0 + 512]
                else:
                    rows = pgrp[g][:, :, (tp - keep) // dil:, off * 512:(off + 1) * 512]
                    rows = rows.transpose(0, 2, 1, 3)
                acc[name % g].append(rows.reshape(bp, keep, DIL_HEADS, DIL_DH))

        h, hb = _ffn_ln(xs, wg[0], wu[0], wd[0], lng[0], lnb[0])
        proj = _inproj(hb, w_main, tab_s, tm_s, (COL_AQK,), (COL_C0, COL_C0 + 1))
        proj3 = proj.reshape(bs, ts, MAIN_COLS)
        tmaj = lambda lo, hi: proj3[:, :, lo:hi].transpose(1, 0, 2)
        oa, rs = _retention_sample(tmaj(0, 256), tmaj(256, 512), tmaj(512, 1024), state_ret, l)
        ob, gs = _hgrn_sample(tmaj(COL_BQ * 512, COL_BF * 512), tmaj(COL_BF * 512, COL_BI * 512),
                              tmaj(COL_BI * 512, COL_BG * 512), lb, state_hgrn, l)
        oa = oa.transpose(1, 0, 2).reshape(ns, 512)
        ob = ob.transpose(1, 0, 2).reshape(ns, 512)
        qkv_s = [proj[:, COL_C0 * 512:(COL_C0 + 3) * 512]]
        qkv_s += [_inproj(hb, w_grp[g], tab_s, tm_s, (), (0, 1)) for g in range(1, N_GROUPS)]
        new9 = jnp.stack([qkv_s[g][:, c * 512:(c + 1) * 512].reshape(bs, ts, DIL_HEADS, DIL_DH)
                          for c in range(3) for g in range(N_GROUPS)], axis=2)
        oc = _dilated_sample(new9, caches, l).reshape(ns, 512)
        h = _merge(h, hb, proj, oa, ob, [oc], w_gate, gn, hn, wb, wo, lng[1], lnb[1])
        q = _matmul(h, wq, tm=tm_s)
        xo = _xattn_sample(q.reshape(bs, ts * X_HEADS, X_DH), cache_mem_k, cache_mem_v, l)
        h = _matmul_res_ln(xo.reshape(ns, D_MODEL), wxo, h, lng[2], lnb[2], tm=tm_s)
        xs, _ = _ffn_ln(h, wg[1], wu[1], wd[1], lng[3], lnb[3])

        acc["ret_s"].append(rs)
        acc["hg_s"].append(gs)
        for g in range(N_GROUPS):
            acc["wk%d_s" % g].append(new9[:, :, 3 + g])
            acc["wv%d_s" % g].append(new9[:, :, 6 + g])

    st = {k: jnp.stack(v, axis=0) for k, v in acc.items()}
    return (xp.reshape(bp, tp, D_MODEL), xs.reshape(bs, ts, D_MODEL),
            st["ret_p"], st["hg_p"],
            st["wk0_p"], st["wv0_p"], st["wk1_p"], st["wv1_p"], st["wk2_p"], st["wv2_p"],
            st["mk_p"], st["mv_p"],
            st["ret_s"], st["hg_s"],
            st["wk0_s"], st["wv0_s"], st["wk1_s"], st["wv1_s"], st["wk2_s"], st["wv2_s"])
```

```python
import functools
import math

import numpy as np
import jax
import jax.numpy as jnp
from jax import lax
from jax.experimental import pallas as pl
from jax.experimental.pallas import tpu as pltpu

D_MODEL = 1024
DEPTH = 2
PAST_LEN = 2048
D_FF = 2816
BRANCH_W = 512
RET_HEADS = 4
RET_DK = 64
RET_DV = 128
RET_THETA = 10000.0
RET_CHUNK = 128
HG_HEADS = 4
HG_DK = 128
HG_DV = 128
HG_CHUNK = 64
DIL_GROUPS = ((128, 1), (512, 4), (2048, 16))
N_GROUPS = 3
DIL_HEADS = 8
DIL_DH = 64
ROPE_DIM = DIL_DH // 4
ROPE_THETA = 500000.0
DIL_QB = 128
MEM_LEN = 256
X_HEADS = 4
X_DH = D_MODEL // X_HEADS
IN_COLS = 11264
ALPHA = (2 * DEPTH) ** 0.25
EPS = 1e-5
NEG = -1e30
EXP_CLIP = 80.0
F32 = jnp.float32
BF16 = jnp.bfloat16

COL_AQK = 0
COL_AV = 1
COL_AG = 2
COL_BQ = 3
COL_BF = 4
COL_BI = 5
COL_BG = 6
COL_C0 = 7
MAIN_COLS = 10 * 512
W_IN_CQ, W_IN_CK, W_IN_CV, W_IN_GATE = 3584, 5120, 6656, 8192

VMEM_LIMIT = 56 * 1024 * 1024


def _cparams(*sem):
    return pltpu.CompilerParams(dimension_semantics=sem, vmem_limit_bytes=VMEM_LIMIT)


def _dot(a, b):
    return jnp.dot(a, b, preferred_element_type=F32)


def _dot_nt(a, b):
    return lax.dot_general(a, b, (((1,), (1,)), ((), ())), preferred_element_type=F32)


def _dot_tn(a, b):
    return lax.dot_general(a, b, (((0,), (0,)), ((), ())), preferred_element_type=F32)


def _ln(y, g, b):
    yc = y - jnp.mean(y, -1, keepdims=True)
    var = jnp.mean(yc * yc, -1, keepdims=True)
    return yc * lax.rsqrt(var + EPS) * g + b


def _silu(x):
    return x * jax.nn.sigmoid(x)


def _ffn_kernel(x_ref, wg_ref, wu_ref, wd_ref, g_ref, b_ref, o_ref, ob_ref, acc_ref, xb_ref):
    j = pl.program_id(1)

    @pl.when(j == 0)
    def _():
        acc_ref[...] = jnp.zeros_like(acc_ref)
        xb_ref[...] = x_ref[...].astype(BF16)

    xb = xb_ref[...]
    hg = _dot(xb, wg_ref[...])
    hu = _dot(xb, wu_ref[...])
    hid = (_silu(hg) * hu).astype(BF16)
    acc_ref[...] += _dot(hid, wd_ref[...])

    @pl.when(j == pl.num_programs(1) - 1)
    def _():
        y = _ln(ALPHA * x_ref[...] + 0.5 * acc_ref[...], g_ref[...], b_ref[...])
        o_ref[...] = y
        ob_ref[...] = y.astype(BF16)


def _ffn_ln(x, wg, wu, wd, g, b, tm=512, tf=D_FF // 2):
    n = x.shape[0]
    row = pl.BlockSpec((tm, D_MODEL), lambda i, j: (i, 0))
    return pl.pallas_call(
        _ffn_kernel,
        grid=(n // tm, D_FF // tf),
        in_specs=[
            row,
            pl.BlockSpec((D_MODEL, tf), lambda i, j: (0, j)),
            pl.BlockSpec((D_MODEL, tf), lambda i, j: (0, j)),
            pl.BlockSpec((tf, D_MODEL), lambda i, j: (j, 0)),
            pl.BlockSpec((1, D_MODEL), lambda i, j: (0, 0)),
            pl.BlockSpec((1, D_MODEL), lambda i, j: (0, 0)),
        ],
        out_specs=[row, row],
        out_shape=[jax.ShapeDtypeStruct((n, D_MODEL), F32), jax.ShapeDtypeStruct((n, D_MODEL), BF16)],
        scratch_shapes=[pltpu.VMEM((tm, D_MODEL), F32), pltpu.VMEM((tm, D_MODEL), BF16)],
        compiler_params=_cparams("parallel", "arbitrary"),
        name="ffn_ln",
    )(x, wg, wu, wd, g, b)


def _rot128(y, c, s_lo, s_hi, half):
    return y * c + pltpu.roll(y, 128 - half, 1) * s_lo + pltpu.roll(y, half, 1) * s_hi


def _inproj_kernel(x_ref, w_ref, tab_ref, o_ref, *, ret_blocks, dil_blocks):
    j = pl.program_id(1)
    o_ref[...] = _dot(x_ref[...], w_ref[...])

    def among(blocks):
        hit = j == blocks[0]
        for c in blocks[1:]:
            hit = jnp.logical_or(hit, j == c)
        return hit

    for blocks, t0, half in ((ret_blocks, 0, RET_DK // 2), (dil_blocks, 3, ROPE_DIM // 2)):
        if not blocks:
            continue

        @pl.when(among(blocks))
        def _(t0=t0, half=half):
            for c in range(4):
                sl = slice(c * 128, (c + 1) * 128)
                o_ref[:, sl] = _rot128(o_ref[:, sl], tab_ref[t0], tab_ref[t0 + 1], tab_ref[t0 + 2], half)


def _inproj(x, w, tabs, tm, ret_blocks, dil_blocks):
    n = x.shape[0]
    cols = w.shape[1]
    nt = tabs.shape[1] // tm
    return pl.pallas_call(
        functools.partial(_inproj_kernel, ret_blocks=ret_blocks, dil_blocks=dil_blocks),
        grid=(n // tm, cols // 512),
        in_specs=[
            pl.BlockSpec((tm, D_MODEL), lambda i, j: (i, 0)),
            pl.BlockSpec((D_MODEL, 512), lambda i, j: (0, j)),
            pl.BlockSpec((6, tm, 128), lambda i, j: (0, i % nt, 0)),
        ],
        out_specs=pl.BlockSpec((tm, 512), lambda i, j: (i, j)),
        out_shape=jax.ShapeDtypeStruct((n, cols), F32),
        compiler_params=_cparams("parallel", "arbitrary"),
        name="in_proj_rotary",
    )(x, w, tabs)


def _mm_kernel(x_ref, w_ref, o_ref):
    o_ref[...] = _dot(x_ref[...].astype(BF16), w_ref[...])


def _matmul(x, w, tm, tn=512):
    n, k = x.shape
    m = w.shape[1]
    return pl.pallas_call(
        _mm_kernel,
        grid=(n // tm, m // tn),
        in_specs=[pl.BlockSpec((tm, k), lambda i, j: (i, 0)),
                  pl.BlockSpec((k, tn), lambda i, j: (0, j))],
        out_specs=pl.BlockSpec((tm, tn), lambda i, j: (i, j)),
        out_shape=jax.ShapeDtypeStruct((n, m), F32),
        compiler_params=_cparams("parallel", "arbitrary"),
        name="matmul",
    )(x, w)


def _mm_res_ln_kernel(x_ref, w_ref, r_ref, g_ref, b_ref, o_ref):
    y = _dot(x_ref[...].astype(BF16), w_ref[...])
    o_ref[...] = _ln(ALPHA * r_ref[...] + y, g_ref[...], b_ref[...])


def _matmul_res_ln(x, w, res, g, b, tm):
    n = x.shape[0]
    return pl.pallas_call(
        _mm_res_ln_kernel,
        grid=(n // tm,),
        in_specs=[pl.BlockSpec((tm, D_MODEL), lambda i: (i, 0)),
                  pl.BlockSpec((D_MODEL, D_MODEL), lambda i: (0, 0)),
                  pl.BlockSpec((tm, D_MODEL), lambda i: (i, 0)),
                  pl.BlockSpec((1, D_MODEL), lambda i: (0, 0)),
                  pl.BlockSpec((1, D_MODEL), lambda i: (0, 0))],
        out_specs=pl.BlockSpec((tm, D_MODEL), lambda i: (i, 0)),
        out_shape=jax.ShapeDtypeStruct((n, D_MODEL), F32),
        compiler_params=_cparams("parallel"),
        name="proj_res_ln",
    )(x, w, res, g, b)


def _ret_gammas():
    return [1.0 - 2.0 ** (-5.0 - h) for h in range(RET_HEADS)]


def _ret_kernel(qk_ref, v_ref, s0_ref, din_ref, dq_ref, dk_ref, o_ref, so_ref, r_scr, *, nchunk, dc):
    c = RET_CHUNK

    @pl.when(pl.program_id(1) == 0)
    def _():
        r_scr[...] = s0_ref[0]

    def chunk(ci, carry):
        row = pl.multiple_of(ci * c, c)
        qk = qk_ref[0, pl.ds(row, c), :]
        v = v_ref[0, pl.ds(row, c), :]
        for h in range(RET_HEADS):
            q = qk[:, h * RET_DK:(h + 1) * RET_DK] * (RET_DK ** -0.5)
            k = qk[:, 256 + h * RET_DK:256 + (h + 1) * RET_DK]
            vh = v[:, h * RET_DV:(h + 1) * RET_DV].astype(BF16)
            r = r_scr[h]
            s = _dot_nt(q.astype(BF16), k.astype(BF16)) * din_ref[h]
            o = _dot(s.astype(BF16), vh) + _dot((q * dq_ref[h]).astype(BF16), r.astype(BF16))
            r_scr[h] = r * dc[h] + _dot_tn((k * dk_ref[h]).astype(BF16), vh)
            o_ref[0, pl.ds(row, c), h * RET_DV:(h + 1) * RET_DV] = o
        return carry

    lax.fori_loop(0, nchunk, chunk, 0)

    @pl.when(pl.program_id(1) == pl.num_programs(1) - 1)
    def _():
        so_ref[0] = r_scr[...]


def _retention_prompt(proj3, s0, tt=1024):
    b, t, _ = proj3.shape
    c = RET_CHUNK
    gam = np.array(_ret_gammas(), np.float64)
    i = np.arange(c, dtype=np.float64)
    diff = i[:, None] - i[None, :]
    din = np.where(diff >= 0, gam[:, None, None] ** np.maximum(diff, 0.0), 0.0)
    dq = np.broadcast_to((gam[:, None] ** (i + 1.0))[:, :, None], (RET_HEADS, c, RET_DK))
    dk = np.broadcast_to((gam[:, None] ** (c - 1.0 - i))[:, :, None], (RET_HEADS, c, RET_DK))
    dc = tuple(float(g ** c) for g in gam)
    const = lambda shape: pl.BlockSpec(shape, lambda bi, ti: (0,) * len(shape))
    return pl.pallas_call(
        functools.partial(_ret_kernel, nchunk=tt // c, dc=dc),
        grid=(b, t // tt),
        in_specs=[
            pl.BlockSpec((1, tt, 512), lambda bi, ti: (bi, ti, COL_AQK)),
            pl.BlockSpec((1, tt, 512), lambda bi, ti: (bi, ti, COL_AV)),
            pl.BlockSpec((1, RET_HEADS, RET_DK, RET_DV), lambda bi, ti: (bi, 0, 0, 0)),
            const((RET_HEADS, c, c)), const((RET_HEADS, c, RET_DK)), const((RET_HEADS, c, RET_DK)),
        ],
        out_specs=[
            pl.BlockSpec((1, tt, 512), lambda bi, ti: (bi, ti, 0)),
            pl.BlockSpec((1, RET_HEADS, RET_DK, RET_DV), lambda bi, ti: (bi, 0, 0, 0)),
        ],
        out_shape=[jax.ShapeDtypeStruct((b, t, BRANCH_W), F32),
                   jax.ShapeDtypeStruct((b, RET_HEADS, RET_DK, RET_DV), F32)],
        scratch_shapes=[pltpu.VMEM((RET_HEADS, RET_DK, RET_DV), F32)],
        compiler_params=_cparams("parallel", "arbitrary"),
        name="retention_chunked",
    )(proj3, proj3, s0, jnp.asarray(din, F32), jnp.asarray(dq, F32), jnp.asarray(dk, F32))


def _hgrn_log_f(z, lb):
    log_sig = jnp.minimum(z, 0.0) - jnp.log1p(jnp.exp(-jnp.abs(z)))
    return log_sig + jnp.log1p(lb * jnp.exp(jnp.minimum(-z, EXP_CLIP)))


def _cumsum_rows(g):
    n = g.shape[0]
    row = lax.broadcasted_iota(jnp.int32, g.shape, 0)
    sh = 1
    while sh < n:
        g = g + jnp.where(row >= sh, pltpu.roll(g, sh, 0), 0.0)
        sh *= 2
    return g


def _hgrn_kernel(q_ref, f_ref, i_ref, lb_ref, s0_ref, o_ref, so_ref, s_scr, *, nchunk):
    c = HG_CHUNK
    nslab = c // 8

    @pl.when(pl.program_id(2) == 0)
    def _():
        s_scr[...] = s0_ref[0, 0]

    lb = lb_ref[...]
    sub = 16
    rowi = lax.broadcasted_iota(jnp.int32, (8, HG_DK), 0)
    rowc = lax.broadcasted_iota(jnp.int32, (c, HG_DK), 0)

    def chunk(ci, carry):
        row = pl.multiple_of(ci * c, c)
        qh = _silu(q_ref[0, pl.ds(row, c), :]) * (HG_DK ** -0.5)
        g = _hgrn_log_f(f_ref[0, pl.ds(row, c), :], lb)
        v = i_ref[0, pl.ds(row, c), :]
        b = _cumsum_rows(g)
        ki = 1.0 - jnp.exp(g)
        s_mat = s_scr[...]
        vb = v.astype(BF16)
        o = _dot((qh * jnp.exp(b)).astype(BF16), s_mat.astype(BF16))
        a_blocks = [jnp.zeros((sub, c), F32)]
        for blk in range(1, c // sub):
            lo = blk * sub
            r = b[lo - 1:lo]
            qt = qh[lo:lo + sub] * jnp.exp(b[lo:lo + sub] - r)
            kt = jnp.where(rowc < lo, ki * jnp.exp(jnp.minimum(r - b, 0.0)), 0.0)
            a_blocks.append(_dot_nt(qt.astype(BF16), kt.astype(BF16)))
        o = o + _dot(jnp.concatenate(a_blocks, axis=0).astype(BF16), vb)
        o_sl = [o[8 * j:8 * j + 8] for j in range(nslab)]
        b_sl = [b[8 * j:8 * j + 8] for j in range(nslab)]
        q_sl = [qh[8 * j:8 * j + 8] for j in range(nslab)]
        for s in range(c):
            js = s // 8
            bs, ks, vs = b[s:s + 1], ki[s:s + 1], v[s:s + 1]
            for j in range(js, (s // sub + 1) * (sub // 8)):
                d = b_sl[j] - bs
                if j == js:
                    d = jnp.where(rowi >= s % 8, d, NEG)
                a = jnp.sum(q_sl[j] * ks * jnp.exp(d), axis=-1, keepdims=True)
                o_sl[j] = o_sl[j] + a * vs
        o_ref[0, pl.ds(row, c), :] = jnp.concatenate(o_sl, axis=0)
        b_end = b[c - 1:c]
        decay_col = jnp.broadcast_to(jnp.exp(b_end), (HG_DK, HG_DK)).T
        s_scr[...] = s_mat * decay_col + _dot_tn((ki * jnp.exp(b_end - b)).astype(BF16), v.astype(BF16))
        return carry

    lax.fori_loop(0, nchunk, chunk, 0)

    @pl.when(pl.program_id(2) == pl.num_programs(2) - 1)
    def _():
        so_ref[0, 0] = s_scr[...]


def _hgrn_prompt(proj3, lb, s0, tt=1024):
    b, t, _ = proj3.shape
    col = lambda base: (lambda bi, h, ti: (bi, ti, base * 4 + h))
    return pl.pallas_call(
        functools.partial(_hgrn_kernel, nchunk=tt // HG_CHUNK),
        grid=(b, HG_HEADS, t // tt),
        in_specs=[
            pl.BlockSpec((1, tt, 128), col(COL_BQ)),
            pl.BlockSpec((1, tt, 128), col(COL_BF)),
            pl.BlockSpec((1, tt, 128), col(COL_BI)),
            pl.BlockSpec((1, 128), lambda bi, h, ti: (0, h)),
            pl.BlockSpec((1, 1, HG_DK, HG_DV), lambda bi, h, ti: (bi, h, 0, 0)),
        ],
        out_specs=[
            pl.BlockSpec((1, tt, 128), lambda bi, h, ti: (bi, ti, h)),
            pl.BlockSpec((1, 1, HG_DK, HG_DV), lambda bi, h, ti: (bi, h, 0, 0)),
        ],
        out_shape=[jax.ShapeDtypeStruct((b, t, BRANCH_W), F32),
                   jax.ShapeDtypeStruct((b, HG_HEADS, HG_DK, HG_DV), F32)],
        scratch_shapes=[pltpu.VMEM((HG_DK, HG_DV), F32)],
        compiler_params=_cparams("parallel", "parallel", "arbitrary"),
        name="hgrn2_chunked",
    )(proj3, proj3, proj3, lb, s0)


def _dil_kernel(q_ref, kc_ref, kp_ref, vc_ref, vp_ref, o_ref, lse_ref, s_scr, p_scr):
    i = pl.program_id(2)
    qb = DIL_QB
    npair = DIL_HEADS // 2
    pairs = [slice(c * 128, (c + 1) * 128) for c in range(npair)]
    first = lax.broadcasted_iota(jnp.int32, (qb, 128), 1) < DIL_DH
    t = lax.broadcasted_iota(jnp.int32, (qb, qb), 0)
    s = lax.broadcasted_iota(jnp.int32, (qb, qb), 1)
    for u in range(q_ref.shape[0] // qb):
        rows = slice(u * qb, (u + 1) * qb)
        if u == 0:
            k_prev, v_prev, has_prev = kp_ref, vp_ref, i > 0
        else:
            prev_rows = slice((u - 1) * qb, u * qb)
            k_prev, v_prev, has_prev = kc_ref.at[prev_rows], vc_ref.at[prev_rows], True
        for c, psl in enumerate(pairs):
            q = q_ref[rows, psl] * (DIL_DH ** -0.5)
            kp = k_prev[:, psl].astype(BF16)
            kc = kc_ref[rows, psl].astype(BF16)
            for half in range(2):
                qm = jnp.where(first if half == 0 else jnp.logical_not(first), q, 0.0).astype(BF16)
                s_scr[2 * c + half, :, 0:qb] = _dot_nt(qm, kp)
                s_scr[2 * c + half, :, qb:2 * qb] = _dot_nt(qm, kc)
        in_prev = jnp.logical_and(s >= t, has_prev)
        in_cur = s <= t
        for c, psl in enumerate(pairs):
            lses = []
            for half in range(2):
                h = 2 * c + half
                sp = jnp.where(in_prev, s_scr[h, :, 0:qb], NEG)
                sc = jnp.where(in_cur, s_scr[h, :, qb:2 * qb], NEG)
                m = jnp.max(jnp.maximum(sp, sc), -1, keepdims=True)
                pp = jnp.exp(sp - m)
                pc = jnp.exp(sc - m)
                l = jnp.sum(pp + pc, -1, keepdims=True)
                p_scr[h, :, 0:qb] = (pp / l).astype(BF16)
                p_scr[h, :, qb:2 * qb] = (pc / l).astype(BF16)
                lses.append(m + jnp.log(l))
            lse_ref[rows, psl] = jnp.where(first, lses[0], lses[1])
        for c, psl in enumerate(pairs):
            vp = v_prev[:, psl].astype(BF16)
            vc = vc_ref[rows, psl].astype(BF16)
            o2 = [_dot(p_scr[2 * c + half, :, 0:qb], vp) + _dot(p_scr[2 * c + half, :, qb:2 * qb], vc)
                  for half in range(2)]
            o_ref[rows, psl] = jnp.where(first, o2[0], o2[1])


def _dilated_prompt(qkv4, col0, g):
    b, dil, tr, _ = qkv4.shape
    step = min(4 * DIL_QB, tr)
    nsub = step // DIL_QB
    assert dil == DIL_GROUPS[g][1] and tr % step == 0 and step % DIL_QB == 0
    cur = lambda c: (lambda bi, r, i: (bi, r, i, c))
    prev = lambda c: (lambda bi, r, i: (bi, r, jnp.maximum(i * nsub - 1, 0), c))
    blk = (None, None, step, 512)
    pblk = (None, None, DIL_QB, 512)
    oblk = (None, step, 512)
    o, lse = pl.pallas_call(
        _dil_kernel,
        grid=(b, dil, tr // step),
        in_specs=[pl.BlockSpec(blk, cur(col0)),
                  pl.BlockSpec(blk, cur(col0 + 1)), pl.BlockSpec(pblk, prev(col0 + 1)),
                  pl.BlockSpec(blk, cur(col0 + 2)), pl.BlockSpec(pblk, prev(col0 + 2))],
        out_specs=[pl.BlockSpec(oblk, lambda bi, r, i: (bi, i, r)),
                   pl.BlockSpec(oblk, lambda bi, r, i: (bi, i, r))],
        out_shape=[jax.ShapeDtypeStruct((b, tr, dil * 512), F32),
                   jax.ShapeDtypeStruct((b, tr, dil * 512), F32)],
        scratch_shapes=[pltpu.VMEM((DIL_HEADS, DIL_QB, 2 * DIL_QB), F32),
                        pltpu.VMEM((DIL_HEADS, DIL_QB, 2 * DIL_QB), BF16)],
        compiler_params=_cparams("parallel", "parallel", "arbitrary"),
        name="dilated_attn_%d" % g,
    )(qkv4, qkv4, qkv4, qkv4, qkv4)
    return o.reshape(b * tr * dil, 512), lse.reshape(b * tr * dil, 512)


def _head_norm(x, centre):
    parts = []
    for h in range(4):
        xh = x[:, h * 128:(h + 1) * 128]
        if centre:
            xh = xh - jnp.mean(xh, -1, keepdims=True)
        parts.append(xh * lax.rsqrt(jnp.mean(xh * xh, -1, keepdims=True) + EPS))
    return jnp.concatenate(parts, axis=1)


def _merge_kernel(*refs, n_groups):
    h_ref, hb_ref, oa_ref, ag_ref, ob_ref, bg_ref = refs[:6]
    c_refs = refs[6:6 + (1 if n_groups == 1 else 2 * n_groups)]
    wgate_ref, gn_ref, hn_ref, wb_ref, wo_ref, lg_ref, lb_ref, o_ref = refs[6 + len(c_refs):]
    if n_groups == 1:
        oc = c_refs[0][...]
    else:
        ls = [r[...] for r in c_refs[n_groups:]]
        m = functools.reduce(jnp.maximum, ls)
        es = [jnp.exp(x - m) for x in ls]
        den = functools.reduce(lambda a, b: a + b, es)
        oc = functools.reduce(lambda a, b: a + b, [(e / den) * r[...] for e, r in zip(es, c_refs[:n_groups])])
    oa = _head_norm(oa_ref[...], True) * gn_ref[...] * _silu(ag_ref[...])
    ob = _head_norm(ob_ref[...], False) * hn_ref[...] * jax.nn.sigmoid(bg_ref[...])
    hb = hb_ref[...]
    merged = None
    for br, x in enumerate((oa, ob, oc)):
        gate = jax.nn.sigmoid(_dot(hb, wgate_ref[:, br * D_MODEL:(br + 1) * D_MODEL]))
        term = gate * _dot(x.astype(BF16), wb_ref[br])
        merged = term if merged is None else merged + term
    mix = _dot(merged.astype(BF16), wo_ref[...])
    o_ref[...] = _ln(ALPHA * h_ref[...] + mix, lg_ref[...], lb_ref[...])


def _merge(h, hb, proj, oa, ob, c_parts, wgate, gn, hn, wb, wo, lg, lb, tm=256):
    n = h.shape[0]
    n_groups = 1 if len(c_parts) == 1 else len(c_parts) // 2
    row512 = lambda cb: pl.BlockSpec((tm, 512), lambda i: (i, cb))
    row1024 = pl.BlockSpec((tm, D_MODEL), lambda i: (i, 0))
    const = lambda shape: pl.BlockSpec(shape, lambda i: (0,) * len(shape))
    return pl.pallas_call(
        functools.partial(_merge_kernel, n_groups=n_groups),
        grid=(n // tm,),
        in_specs=[row1024, row1024, row512(0), row512(COL_AG), row512(0), row512(COL_BG)]
                 + [row512(0)] * len(c_parts)
                 + [const((D_MODEL, 3 * D_MODEL)), const((1, 512)), const((1, 512)),
                    const((3, 512, D_MODEL)), const((D_MODEL, D_MODEL)),
                    const((1, D_MODEL)), const((1, D_MODEL))],
        out_specs=row1024,
        out_shape=jax.ShapeDtypeStruct((n, D_MODEL), F32),
        compiler_params=_cparams("parallel"),
        name="branch_merge",
    )(h, hb, oa, proj, ob, proj, *c_parts, wgate, gn, hn, wb, wo, lg, lb)


def _xattn_block_kernel(h_ref, wq_ref, k_ref, v_ref, wo_ref, g_ref, b_ref, o_ref):
    h = h_ref[0]
    q = _dot(h.astype(BF16), wq_ref[...])
    outs = []
    for hd in range(X_HEADS):
        sl = slice(hd * X_DH, (hd + 1) * X_DH)
        s = _dot_nt(q[:, sl].astype(BF16), k_ref[0, :, sl]) * (X_DH ** -0.5)
        m = jnp.max(s, -1, keepdims=True)
        p = jnp.exp(s - m)
        p = p / jnp.sum(p, -1, keepdims=True)
        outs.append(_dot(p.astype(BF16), v_ref[0, :, sl]).astype(BF16))
    y = _dot(jnp.concatenate(outs, axis=1), wo_ref[...])
    o_ref[0] = _ln(ALPHA * h + y, g_ref[...], b_ref[...])


def _xattn_block(h3, wq, mk, mv, wo, g, b, tq=512):
    bsz, t, _ = h3.shape
    row = pl.BlockSpec((1, tq, D_MODEL), lambda bi, ti: (bi, ti, 0))
    mem = pl.BlockSpec((1, MEM_LEN, D_MODEL), lambda bi, ti: (bi, 0, 0))
    wspec = pl.BlockSpec((D_MODEL, D_MODEL), lambda bi, ti: (0, 0))
    vec = pl.BlockSpec((1, D_MODEL), lambda bi, ti: (0, 0))
    return pl.pallas_call(
        _xattn_block_kernel,
        grid=(bsz, t // tq),
        in_specs=[row, wspec, mem, mem, wspec, vec, vec],
        out_specs=row,
        out_shape=jax.ShapeDtypeStruct((bsz, t, D_MODEL), F32),
        compiler_params=_cparams("parallel", "arbitrary"),
        name="cross_attn_block",
    )(h3, wq, mk, mv, wo, g, b)


def _xattn_s_kernel(q_ref, k_ref, v_ref, o_ref, *, bb):
    nq = q_ref.shape[1]
    nk = MEM_LEN * X_HEADS
    same_head = (lax.broadcasted_iota(jnp.int32, (nq, nk), 0) % X_HEADS
                 == lax.broadcasted_iota(jnp.int32, (nq, nk), 1) % X_HEADS)
    for bi in range(bb):
        k2 = k_ref[bi].reshape(nk, X_DH).astype(BF16)
        v2 = v_ref[bi].reshape(nk, X_DH).astype(BF16)
        s = _dot_nt(q_ref[bi].astype(BF16), k2) * (X_DH ** -0.5)
        s = jnp.where(same_head, s, NEG)
        m = jnp.max(s, -1, keepdims=True)
        p = jnp.exp(s - m)
        p = p / jnp.sum(p, -1, keepdims=True)
        o_ref[bi] = _dot(p.astype(BF16), v2)


def _xattn_sample(q3, mk5, mv5, layer, bb=2):
    b, nq, _ = q3.shape
    mem_spec = pl.BlockSpec((None, bb, MEM_LEN, X_HEADS, X_DH), lambda bi: (layer, bi, 0, 0, 0))
    row = pl.BlockSpec((bb, nq, X_DH), lambda bi: (bi, 0, 0))
    return pl.pallas_call(
        functools.partial(_xattn_s_kernel, bb=bb),
        grid=(b // bb,),
        in_specs=[row, mem_spec, mem_spec],
        out_specs=row,
        out_shape=jax.ShapeDtypeStruct((b, nq, X_DH), F32),
        compiler_params=_cparams("parallel"),
        name="cross_attn_step",
    )(q3, mk5, mv5)


def _pick_col(tile, onehot):
    return jnp.sum(jnp.where(onehot, tile, 0.0), axis=1, keepdims=True)


def _pick_row(slab, sub_hit):
    return jnp.sum(jnp.where(sub_hit, slab, 0.0), axis=0, keepdims=True)


def _ret_s_kernel(q_ref, k_ref, v_ref, s_ref, o_ref, so_ref, qt_scr, kt_scr, *, nt, bb):
    j = pl.program_id(0)
    nb = q_ref.shape[1]

    @pl.when(j == 0)
    def _():
        for t in range(nt):
            for c in range(2):
                sl = slice(c * 128, (c + 1) * 128)
                qt_scr[t, sl, :] = (q_ref[t, :, sl] * (RET_DK ** -0.5)).T
                kt_scr[t, sl, :] = k_ref[t, :, sl].T

    lane = lax.broadcasted_iota(jnp.int32, (RET_DK, nb), 1)
    sub = lax.broadcasted_iota(jnp.int32, (8, 128), 0)
    o_ref[...] = jnp.zeros_like(o_ref)
    gam = _ret_gammas()

    def body(bi, carry):
        bg = j * bb + bi
        onehot = lane == bg
        row_g = pl.multiple_of((bg // 8) * 8, 8)
        row_l = pl.multiple_of((bi // 8) * 8, 8)
        sub_hit = sub == bi % 8
        for h in range(RET_HEADS):
            r = s_ref[bi, h]
            for t in range(nt):
                ksl = slice(h * RET_DK, (h + 1) * RET_DK)
                qc = _pick_col(qt_scr[t, ksl, :], onehot)
                kc = _pick_col(kt_scr[t, ksl, :], onehot)
                vsl = slice(h * RET_DV, (h + 1) * RET_DV)
                vrow = _pick_row(v_ref[t, pl.ds(row_g, 8), vsl], sub_hit)
                r = r * gam[h] + kc * vrow
                orow = jnp.sum(r * qc, axis=0, keepdims=True)
                o_ref[t, pl.ds(row_l, 8), vsl] = jnp.where(sub_hit, orow, o_ref[t, pl.ds(row_l, 8), vsl])
            so_ref[bi, h] = r
        return carry

    lax.fori_loop(0, bb, body, 0)


def _retention_sample(q_t, k_t, v_t, state, layer, bb=16):
    nt, nb, _ = q_t.shape
    full = lambda w: pl.BlockSpec((nt, nb, w), lambda j: (0, 0, 0))
    return pl.pallas_call(
        functools.partial(_ret_s_kernel, nt=nt, bb=bb),
        grid=(nb // bb,),
        in_specs=[full(256), full(256), full(512),
                  pl.BlockSpec((None, bb, RET_HEADS, RET_DK, RET_DV), lambda j: (layer, j, 0, 0, 0))],
        out_specs=[pl.BlockSpec((nt, bb, 512), lambda j: (0, j, 0)),
                   pl.BlockSpec((bb, RET_HEADS, RET_DK, RET_DV), lambda j: (j, 0, 0, 0))],
        out_shape=[jax.ShapeDtypeStruct((nt, nb, 512), F32),
                   jax.ShapeDtypeStruct((nb, RET_HEADS, RET_DK, RET_DV), F32)],
        scratch_shapes=[pltpu.VMEM((nt, 256, nb), F32), pltpu.VMEM((nt, 256, nb), F32)],
        compiler_params=_cparams("arbitrary"),
        name="retention_step",
    )(q_t, k_t, v_t, state)


def _hgrn_s_kernel(q_ref, f_ref, i_ref, lb_ref, s_ref, o_ref, so_ref, qt_scr, ft_scr, kt_scr, *, nt, bb):
    j = pl.program_id(0)
    nb = q_ref.shape[1]

    @pl.when(j == 0)
    def _():
        for t in range(nt):
            for h in range(HG_HEADS):
                sl = slice(h * HG_DK, (h + 1) * HG_DK)
                f = jnp.exp(_hgrn_log_f(f_ref[t, :, sl], lb_ref[:, sl]))
                qt_scr[t, sl, :] = (_silu(q_ref[t, :, sl]) * (HG_DK ** -0.5)).T
                ft_scr[t, sl, :] = f.T
                kt_scr[t, sl, :] = (1.0 - f).T

    lane = lax.broadcasted_iota(jnp.int32, (HG_DK, nb), 1)
    sub = lax.broadcasted_iota(jnp.int32, (8, 128), 0)
    o_ref[...] = jnp.zeros_like(o_ref)

    def body(bi, carry):
        bg = j * bb + bi
        onehot = lane == bg
        row_g = pl.multiple_of((bg // 8) * 8, 8)
        row_l = pl.multiple_of((bi // 8) * 8, 8)
        sub_hit = sub == bi % 8
        for h in range(HG_HEADS):
            sl = slice(h * HG_DK, (h + 1) * HG_DK)
            s_mat = s_ref[bi, h]
            for t in range(nt):
                qc = _pick_col(qt_scr[t, sl, :], onehot)
                fc = _pick_col(ft_scr[t, sl, :], onehot)
                kc = _pick_col(kt_scr[t, sl, :], onehot)
                vrow = _pick_row(i_ref[t, pl.ds(row_g, 8), sl], sub_hit)
                s_mat = s_mat * fc + kc * vrow
                orow = jnp.sum(s_mat * qc, axis=0, keepdims=True)
                o_ref[t, pl.ds(row_l, 8), sl] = jnp.where(sub_hit, orow, o_ref[t, pl.ds(row_l, 8), sl])
            so_ref[bi, h] = s_mat
        return carry

    lax.fori_loop(0, bb, body, 0)


def _hgrn_sample(q_t, f_t, i_t, lb, state, layer, bb=16):
    nt, nb, _ = q_t.shape
    full = pl.BlockSpec((nt, nb, 512), lambda j: (0, 0, 0))
    return pl.pallas_call(
        functools.partial(_hgrn_s_kernel, nt=nt, bb=bb),
        grid=(nb // bb,),
        in_specs=[full, full, full, pl.BlockSpec((1, 512), lambda j: (0, 0)),
                  pl.BlockSpec((None, bb, HG_HEADS, HG_DK, HG_DV), lambda j: (layer, j, 0, 0, 0))],
        out_specs=[pl.BlockSpec((nt, bb, 512), lambda j: (0, j, 0)),
                   pl.BlockSpec((bb, HG_HEADS, HG_DK, HG_DV), lambda j: (j, 0, 0, 0))],
        out_shape=[jax.ShapeDtypeStruct((nt, nb, 512), F32),
                   jax.ShapeDtypeStruct((nb, HG_HEADS, HG_DK, HG_DV), F32)],
        scratch_shapes=[pltpu.VMEM((nt, 512, nb), F32)] * 3,
        compiler_params=_cparams("arbitrary"),
        name="hgrn2_step",
    )(q_t, f_t, i_t, lb, state)


def _dil_s_kernel(new_ref, k0_ref, v0_ref, k1_ref, v1_ref, k2_ref, v2_ref, o_ref, qpad_scr, opad_scr, *, nt):
    caches = ((k0_ref, v0_ref), (k1_ref, v1_ref), (k2_ref, v2_ref))
    scale = DIL_DH ** -0.5
    lane = lax.broadcasted_iota(jnp.int32, (DIL_DH, 128), 1)
    sub8 = lax.broadcasted_iota(jnp.int32, (DIL_HEADS, 128), 0)
    lane8 = lax.broadcasted_iota(jnp.int32, (DIL_HEADS, 128), 1)
    col_id = lambda g, i, h: (g * nt + i) * DIL_HEADS + h

    qpad_scr[...] = jnp.zeros_like(qpad_scr)
    opad_scr[...] = jnp.zeros_like(opad_scr)
    for g in range(N_GROUPS):
        for i in range(nt):
            r0 = col_id(g, i, 0)
            qpad_scr[r0:r0 + DIL_HEADS, 0:DIL_DH] = new_ref[0, i, g]
    qt = qpad_scr[...].T[0:DIL_DH]
    qcol = lambda g, i, h: qt[:, col_id(g, i, h):col_id(g, i, h) + 1]

    oc = jnp.zeros((DIL_DH, 128), F32)
    m_all = [[None] * nt for _ in range(N_GROUPS)]
    l_all = [[None] * nt for _ in range(N_GROUPS)]
    pn_all = [[None] * nt for _ in range(N_GROUPS)]
    for g in range(N_GROUPS):
        kc_ref, vc_ref = caches[g]
        dil = DIL_GROUPS[g][1]
        ntile = kc_ref.shape[-1] // 128
        q_t = [new_ref[0, i, g] for i in range(nt)]
        kn = [new_ref[0, j, 3 + g] for j in range(nt)]
        new_score = lambda i, j: jnp.sum(q_t[i] * kn[j], axis=-1, keepdims=True) * scale
        if dil == 1:
            for i in range(nt):
                s = jnp.zeros((DIL_HEADS, 128), F32)
                for h in range(DIL_HEADS):
                    row = jnp.sum(kc_ref[0, h] * qcol(g, i, h), axis=0, keepdims=True)
                    s = jnp.where(sub8 == h, row, s)
                s = jnp.where(lane8 >= i, s * scale, NEG)
                sn = [new_score(i, j) for j in range(i + 1)]
                m = jnp.max(s, -1, keepdims=True)
                for x in sn:
                    m = jnp.maximum(m, x)
                p = jnp.exp(s - m)
                pn = [jnp.exp(x - m) for x in sn]
                l = jnp.sum(p, -1, keepdims=True)
                for x in pn:
                    l = l + x
                for h in range(DIL_HEADS):
                    col = jnp.sum(vc_ref[0, h] * p[h:h + 1, :], axis=1, keepdims=True)
                    oc = jnp.where(lane == col_id(g, i, h), col, oc)
                m_all[g][i], l_all[g][i], pn_all[g][i] = m, l, list(zip(pn, range(i + 1)))
        else:
            assert nt <= dil and 128 % dil == 0
            cls8 = [(lane8 & (dil - 1)) == i for i in range(nt)]
            cls = [(lane & (dil - 1)) == i for i in range(nt)]
            s_t = [jnp.zeros((DIL_HEADS, 128), F32) for _ in range(ntile)]
            for h in range(DIL_HEADS):
                qsel = jnp.zeros((DIL_DH, 128), F32)
                for i in range(nt):
                    qsel = jnp.where(cls[i], qcol(g, i, h), qsel)
                for t in range(ntile):
                    row = jnp.sum(kc_ref[0, h, :, t * 128:(t + 1) * 128] * qsel, axis=0, keepdims=True)
                    s_t[t] = jnp.where(sub8 == h, row, s_t[t])
            s_t = [s * scale for s in s_t]
            smax = s_t[0]
            for t in range(1, ntile):
                smax = jnp.maximum(smax, s_t[t])
            m_tile = jnp.zeros((DIL_HEADS, 128), F32)
            owned = cls8[0]
            for i in range(nt):
                sn = new_score(i, i)
                m = jnp.maximum(jnp.max(jnp.where(cls8[i], smax, NEG), -1, keepdims=True), sn)
                m_all[g][i] = m
                pn_all[g][i] = [(jnp.exp(sn - m), i)]
                m_tile = jnp.where(cls8[i], m, m_tile)
                owned = jnp.logical_or(owned, cls8[i])
            p_t = [jnp.exp(jnp.where(owned, s - m_tile, NEG)) for s in s_t]
            psum = p_t[0]
            for t in range(1, ntile):
                psum = psum + p_t[t]
            for i in range(nt):
                l_all[g][i] = jnp.sum(jnp.where(cls8[i], psum, 0.0), -1, keepdims=True) + pn_all[g][i][0][0]
            for h in range(DIL_HEADS):
                acc = jnp.zeros((DIL_DH, 128), F32)
                for t in range(ntile):
                    acc = acc + vc_ref[0, h, :, t * 128:(t + 1) * 128] * p_t[t][h:h + 1, :]
                for i in range(nt):
                    col = jnp.sum(jnp.where(cls[i], acc, 0.0), axis=1, keepdims=True)
                    oc = jnp.where(lane == col_id(g, i, h), col, oc)

    opad_scr[0:DIL_DH, :] = oc
    ot = opad_scr[...].T
    for i in range(nt):
        outs, lses = [], []
        for g in range(N_GROUPS):
            r0 = col_id(g, i, 0)
            o = ot[r0:r0 + DIL_HEADS, 0:DIL_DH]
            for pj, j in pn_all[g][i]:
                o = o + pj * new_ref[0, j, 6 + g]
            outs.append(o / l_all[g][i])
            lses.append(m_all[g][i] + jnp.log(l_all[g][i]))
        m = jnp.maximum(jnp.maximum(lses[0], lses[1]), lses[2])
        es = [jnp.exp(x - m) for x in lses]
        den = es[0] + es[1] + es[2]
        o_ref[0, i] = (es[0] / den) * outs[0] + (es[1] / den) * outs[1] + (es[2] / den) * outs[2]


def _dilated_sample(new9, caches, layer):
    nb, nt = new9.shape[:2]
    views, specs = [], []
    for g, (kbuf, vbuf) in enumerate(caches):
        w = kbuf.shape[2]
        assert w == DIL_GROUPS[g][0] and w // DIL_GROUPS[g][1] == 128
        for buf in (kbuf, vbuf):
            views.append(jnp.transpose(buf, (0, 1, 3, 4, 2)))
            specs.append(pl.BlockSpec((None, 1, DIL_HEADS, DIL_DH, w), lambda j: (layer, j, 0, 0, 0)))
    return pl.pallas_call(
        functools.partial(_dil_s_kernel, nt=nt),
        grid=(nb,),
        in_specs=[pl.BlockSpec((1, nt, 9, DIL_HEADS, DIL_DH), lambda j: (j, 0, 0, 0, 0))] + specs,
        out_specs=pl.BlockSpec((1, nt, DIL_HEADS, DIL_DH), lambda j: (j, 0, 0, 0)),
        out_shape=jax.ShapeDtypeStruct((nb, nt, DIL_HEADS, DIL_DH), F32),
        scratch_shapes=[pltpu.VMEM((128, 128), F32), pltpu.VMEM((128, 128), F32)],
        compiler_params=_cparams("parallel"),
        name="dilated_attn_step",
    )(new9, *views)


def _rotary_tables(pos):
    pos = jnp.asarray(np.asarray(pos), jnp.int32)
    lane = np.arange(128) % 64
    out = []
    for rot_dim, theta in ((RET_DK, RET_THETA), (ROPE_DIM, ROPE_THETA)):
        half = rot_dim // 2
        inv = jnp.power(jnp.float32(theta), -jnp.arange(half, dtype=F32) / half)
        ang = pos.astype(F32)[:, None] * inv[None, :]
        idx = lane % half
        cos = jnp.where(lane < rot_dim, jnp.cos(ang)[:, idx], 1.0)
        sin = jnp.sin(ang)[:, idx]
        out += [cos, jnp.where(lane < half, -sin, 0.0),
                jnp.where((lane >= half) & (lane < rot_dim), sin, 0.0)]
    return jnp.stack(out, 0).astype(F32)


def _hgrn_lower_bounds(raw):
    p = jax.nn.softmax(raw.astype(F32), axis=0)
    return jnp.cumsum(p, axis=0) - p[0:1]


def _row(v):
    return v.reshape(1, -1)


def kernel(x_prompt, x_sample, mem_prompt, state_ret, state_hgrn, cache_win_k0, cache_win_v0, cache_win_k1, cache_win_v1, cache_win_k2, cache_win_v2, cache_mem_k, cache_mem_v, ln_g, ln_b, ffn_w_gate, ffn_w_up, ffn_w_down, w_in, ret_gn_g, hgrn_lb_raw, hgrn_norm_g, w_branch, w_out, xattn_w_q, xattn_w_k, xattn_w_v, xattn_w_o):
    bp, tp, _ = x_prompt.shape
    bs, ts, _ = x_sample.shape
    np_, ns = bp * tp, bs * ts
    lb_all = _hgrn_lower_bounds(hgrn_lb_raw)
    tab_p = _rotary_tables(np.arange(tp))
    tab_res = [tab_p] + [_rotary_tables(np.arange(tp).reshape(tp // dil, dil).T.reshape(-1))
                         for _, dil in DIL_GROUPS[1:]]
    tm_s = ns
    tab_s = _rotary_tables(PAST_LEN + (np.arange(tm_s) % ts))
    caches = ((cache_win_k0, cache_win_v0), (cache_win_k1, cache_win_v1), (cache_win_k2, cache_win_v2))
    mem2 = mem_prompt.reshape(bp * MEM_LEN, D_MODEL)

    xp = x_prompt.reshape(np_, D_MODEL)
    xs = x_sample.reshape(ns, D_MODEL)
    tm_p = 1024
    acc = {k: [] for k in ("ret_p", "hg_p", "mk_p", "mv_p", "ret_s", "hg_s")}
    for g in range(N_GROUPS):
        for k in ("wk%d_p", "wv%d_p", "wk%d_s", "wv%d_s"):
            acc[k % g] = []

    for l in range(DEPTH):
        wg = ffn_w_gate[l].astype(BF16)
        wu = ffn_w_up[l].astype(BF16)
        wd = ffn_w_down[l].astype(BF16)
        win = w_in[l].astype(BF16)
        w_main = jnp.concatenate(
            [win[:, :W_IN_CQ], win[:, W_IN_CQ:W_IN_CQ + 512], win[:, W_IN_CK:W_IN_CK + 512],
             win[:, W_IN_CV:W_IN_CV + 512]], axis=1)
        w_gate = win[:, W_IN_GATE:]
        w_grp = [None] + [jnp.concatenate([win[:, o + g * 512:o + (g + 1) * 512]
                                           for o in (W_IN_CQ, W_IN_CK, W_IN_CV)], axis=1)
                          for g in range(1, N_GROUPS)]
        wb = w_branch[l].astype(BF16)
        wo = w_out[l].astype(BF16)
        wq = xattn_w_q[l].astype(BF16)
        wk = xattn_w_k[l].astype(BF16)
        wv = xattn_w_v[l].astype(BF16)
        wxo = xattn_w_o[l].astype(BF16)
        lng = [_row(ln_g[l, i]) for i in range(4)]
        lnb = [_row(ln_b[l, i]) for i in range(4)]
        gn, hn, lb = _row(ret_gn_g[l]), _row(hgrn_norm_g[l]), _row(lb_all[l])

        mk = _matmul(mem2, wk, tm=min(512, bp * MEM_LEN))
        mv = _matmul(mem2, wv, tm=min(512, bp * MEM_LEN))
        h, hb = _ffn_ln(xp, wg[0], wu[0], wd[0], lng[0], lnb[0])
        proj = _inproj(hb, w_main, tab_p, tm_p, (COL_AQK,), (COL_C0, COL_C0 + 1))
        proj3 = proj.reshape(bp, tp, MAIN_COLS)
        oa, rp = _retention_prompt(proj3, jnp.zeros((bp, RET_HEADS, RET_DK, RET_DV), F32))
        ob, gp = _hgrn_prompt(proj3, lb, jnp.zeros((bp, HG_HEADS, HG_DK, HG_DV), F32))
        dres = [_dilated_prompt(proj.reshape(bp, 1, tp, MAIN_COLS), COL_C0, 0)]
        pgrp = [None]
        for g in range(1, N_GROUPS):
            dil = DIL_GROUPS[g][1]
            h_res = hb.reshape(bp, tp // dil, dil, D_MODEL).transpose(0, 2, 1, 3).reshape(np_, D_MODEL)
            pg = _inproj(h_res, w_grp[g], tab_res[g], tm_p, (), (0, 1)).reshape(bp, dil, tp // dil, 1536)
            pgrp.append(pg)
            dres.append(_dilated_prompt(pg, 0, g))
        h = _merge(h, hb, proj, oa.reshape(np_, 512), ob.reshape(np_, 512),
                   [d[0] for d in dres] + [d[1] for d in dres], w_gate, gn, hn, wb, wo, lng[1], lnb[1])
        h = _xattn_block(h.reshape(bp, tp, D_MODEL), wq, mk.astype(BF16).reshape(bp, MEM_LEN, D_MODEL),
                         mv.astype(BF16).reshape(bp, MEM_LEN, D_MODEL), wxo, lng[2], lnb[2])
        xp, _ = _ffn_ln(h.reshape(np_, D_MODEL), wg[1], wu[1], wd[1], lng[3], lnb[3])

        acc["ret_p"].append(rp)
        acc["hg_p"].append(gp)
        acc["mk_p"].append(mk.reshape(bp, MEM_LEN, X_HEADS, X_DH))
        acc["mv_p"].append(mv.reshape(bp, MEM_LEN, X_HEADS, X_DH))
        for g, (window, dil) in enumerate(DIL_GROUPS):
            keep = min(window, tp)
            assert keep % dil == 0
            for name, off in (("wk%d_p", 1), ("wv%d_p", 2)):
                if g == 0:
                    c0 = (COL_C0 + off) * 512
                    rows = proj3[:, tp - keep:, c0:c0 + 512]
                else:
                    rows = pgrp[g][:, :, (tp - keep) // dil:, off * 512:(off + 1) * 512]
                    rows = rows.transpose(0, 2, 1, 3)
                acc[name % g].append(rows.reshape(bp, keep, DIL_HEADS, DIL_DH))

        h, hb = _ffn_ln(xs, wg[0], wu[0], wd[0], lng[0], lnb[0])
        proj = _inproj(hb, w_main, tab_s, tm_s, (COL_AQK,), (COL_C0, COL_C0 + 1))
        proj3 = proj.reshape(bs, ts, MAIN_COLS)
        tmaj = lambda lo, hi: proj3[:, :, lo:hi].transpose(1, 0, 2)
        oa, rs = _retention_sample(tmaj(0, 256), tmaj(256, 512), tmaj(512, 1024), state_ret, l)
        ob, gs = _hgrn_sample(tmaj(COL_BQ * 512, COL_BF * 512), tmaj(COL_BF * 512, COL_BI * 512),
                              tmaj(COL_BI * 512, COL_BG * 512), lb, state_hgrn, l)
        oa = oa.transpose(1, 0, 2).reshape(ns, 512)
        ob = ob.transpose(1, 0, 2).reshape(ns, 512)
        qkv_s = [proj[:, COL_C0 * 512:(COL_C0 + 3) * 512]]
        qkv_s += [_inproj(hb, w_grp[g], tab_s, tm_s, (), (0, 1)) for g in range(1, N_GROUPS)]
        new9 = jnp.stack([qkv_s[g][:, c * 512:(c + 1) * 512].reshape(bs, ts, DIL_HEADS, DIL_DH)
                          for c in range(3) for g in range(N_GROUPS)], axis=2)
        oc = _dilated_sample(new9, caches, l).reshape(ns, 512)
        h = _merge(h, hb, proj, oa, ob, [oc], w_gate, gn, hn, wb, wo, lng[1], lnb[1])
        q = _matmul(h, wq, tm=tm_s)
        xo = _xattn_sample(q.reshape(bs, ts * X_HEADS, X_DH), cache_mem_k, cache_mem_v, l)
        h = _matmul_res_ln(xo.reshape(ns, D_MODEL), wxo, h, lng[2], lnb[2], tm=tm_s)
        xs, _ = _ffn_ln(h, wg[1], wu[1], wd[1], lng[3], lnb[3])

        acc["ret_s"].append(rs)
        acc["hg_s"].append(gs)
        for g in range(N_GROUPS):
            acc["wk%d_s" % g].append(new9[:, :, 3 + g])
            acc["wv%d_s" % g].append(new9[:, :, 6 + g])

    st = {k: jnp.stack(v, axis=0) for k, v in acc.items()}
    return (xp.reshape(bp, tp, D_MODEL), xs.reshape(bs, ts, D_MODEL),
            st["ret_p"], st["hg_p"],
            st["wk0_p"], st["wv0_p"], st["wk1_p"], st["wv1_p"], st["wk2_p"], st["wv2_p"],
            st["mk_p"], st["mv_p"],
            st["ret_s"], st["hg_s"],
            st["wk0_s"], st["wv0_s"], st["wk1_s"], st["wv1_s"], st["wk2_s"], st["wv2_s"])
```

```python
import functools
import math

import numpy as np
import jax
import jax.numpy as jnp
from jax import lax
from jax.experimental import pallas as pl
from jax.experimental.pallas import tpu as pltpu

D_MODEL = 1024
DEPTH = 2
PAST_LEN = 2048
D_FF = 2816
BRANCH_W = 512
RET_HEADS = 4
RET_DK = 64
RET_DV = 128
RET_THETA = 10000.0
RET_CHUNK = 128
HG_HEADS = 4
HG_DK = 128
HG_DV = 128
HG_CHUNK = 64
DIL_GROUPS = ((128, 1), (512, 4), (2048, 16))
N_GROUPS = 3
DIL_HEADS = 8
DIL_DH = 64
ROPE_DIM = DIL_DH // 4
ROPE_THETA = 500000.0
DIL_QB = 128
MEM_LEN = 256
X_HEADS = 4
X_DH = D_MODEL // X_HEADS
IN_COLS = 11264
ALPHA = (2 * DEPTH) ** 0.25
EPS = 1e-5
NEG = -1e30
EXP_CLIP = 80.0
F32 = jnp.float32
BF16 = jnp.bfloat16

COL_AQK = 0
COL_AV = 1
COL_AG = 2
COL_BQ = 3
COL_BF = 4
COL_BI = 5
COL_BG = 6
COL_C0 = 7
MAIN_COLS = 10 * 512
W_IN_CQ, W_IN_CK, W_IN_CV, W_IN_GATE = 3584, 5120, 6656, 8192

VMEM_LIMIT = 56 * 1024 * 1024


def _cparams(*sem):
    return pltpu.CompilerParams(dimension_semantics=sem, vmem_limit_bytes=VMEM_LIMIT)


def _dot(a, b):
    return jnp.dot(a, b, preferred_element_type=F32)


def _dot_nt(a, b):
    return lax.dot_general(a, b, (((1,), (1,)), ((), ())), preferred_element_type=F32)


def _dot_tn(a, b):
    return lax.dot_general(a, b, (((0,), (0,)), ((), ())), preferred_element_type=F32)


def _ln(y, g, b):
    yc = y - jnp.mean(y, -1, keepdims=True)
    var = jnp.mean(yc * yc, -1, keepdims=True)
    return yc * lax.rsqrt(var + EPS) * g + b


def _silu(x):
    return x * jax.nn.sigmoid(x)


def _ffn_kernel(x_ref, wg_ref, wu_ref, wd_ref, g_ref, b_ref, o_ref, ob_ref, acc_ref, xb_ref):
    j = pl.program_id(1)

    @pl.when(j == 0)
    def _():
        acc_ref[...] = jnp.zeros_like(acc_ref)
        xb_ref[...] = x_ref[...].astype(BF16)

    xb = xb_ref[...]
    hg = _dot(xb, wg_ref[...])
    hu = _dot(xb, wu_ref[...])
    hid = (_silu(hg) * hu).astype(BF16)
    acc_ref[...] += _dot(hid, wd_ref[...])

    @pl.when(j == pl.num_programs(1) - 1)
    def _():
        y = _ln(ALPHA * x_ref[...] + 0.5 * acc_ref[...], g_ref[...], b_ref[...])
        o_ref[...] = y
        ob_ref[...] = y.astype(BF16)


def _ffn_ln(x, wg, wu, wd, g, b, tm=512, tf=D_FF // 2):
    n = x.shape[0]
    row = pl.BlockSpec((tm, D_MODEL), lambda i, j: (i, 0))
    return pl.pallas_call(
        _ffn_kernel,
        grid=(n // tm, D_FF // tf),
        in_specs=[
            row,
            pl.BlockSpec((D_MODEL, tf), lambda i, j: (0, j)),
            pl.BlockSpec((D_MODEL, tf), lambda i, j: (0, j)),
            pl.BlockSpec((tf, D_MODEL), lambda i, j: (j, 0)),
            pl.BlockSpec((1, D_MODEL), lambda i, j: (0, 0)),
            pl.BlockSpec((1, D_MODEL), lambda i, j: (0, 0)),
        ],
        out_specs=[row, row],
        out_shape=[jax.ShapeDtypeStruct((n, D_MODEL), F32), jax.ShapeDtypeStruct((n, D_MODEL), BF16)],
        scratch_shapes=[pltpu.VMEM((tm, D_MODEL), F32), pltpu.VMEM((tm, D_MODEL), BF16)],
        compiler_params=_cparams("parallel", "arbitrary"),
        name="ffn_ln",
    )(x, wg, wu, wd, g, b)


def _rot128(y, c, s_lo, s_hi, half):
    return y * c + pltpu.roll(y, 128 - half, 1) * s_lo + pltpu.roll(y, half, 1) * s_hi


def _inproj_kernel(x_ref, w_ref, tab_ref, o_ref, *, ret_blocks, dil_blocks):
    j = pl.program_id(1)

    def among(blocks):
        hit = j == blocks[0]
        for c in blocks[1:]:
            hit = jnp.logical_or(hit, j == c)
        return hit

    plain = None
    for blocks, t0, half in ((ret_blocks, 0, RET_DK // 2), (dil_blocks, 3, ROPE_DIM // 2)):
        if not blocks:
            continue
        hit = among(blocks)
        plain = hit if plain is None else jnp.logical_or(plain, hit)

        @pl.when(hit)
        def _(t0=t0, half=half):
            for c2 in range(2):
                y = _dot(x_ref[...], w_ref[:, c2 * 256:(c2 + 1) * 256])
                for c in range(2):
                    lo = c2 * 256 + c * 128
                    o_ref[:, lo:lo + 128] = _rot128(y[:, c * 128:(c + 1) * 128], tab_ref[t0],
                                                    tab_ref[t0 + 1], tab_ref[t0 + 2], half)

    @pl.when(jnp.logical_not(plain))
    def _():
        o_ref[...] = _dot(x_ref[...], w_ref[...])


def _inproj(x, w, tabs, tm, ret_blocks, dil_blocks):
    n = x.shape[0]
    cols = w.shape[1]
    nt = tabs.shape[1] // tm
    return pl.pallas_call(
        functools.partial(_inproj_kernel, ret_blocks=ret_blocks, dil_blocks=dil_blocks),
        grid=(n // tm, cols // 512),
        in_specs=[
            pl.BlockSpec((tm, D_MODEL), lambda i, j: (i, 0)),
            pl.BlockSpec((D_MODEL, 512), lambda i, j: (0, j)),
            pl.BlockSpec((6, tm, 128), lambda i, j: (0, i % nt, 0)),
        ],
        out_specs=pl.BlockSpec((tm, 512), lambda i, j: (i, j)),
        out_shape=jax.ShapeDtypeStruct((n, cols), F32),
        compiler_params=_cparams("parallel", "arbitrary"),
        name="in_proj_rotary",
    )(x, w, tabs)


def _mm_kernel(x_ref, w_ref, o_ref):
    o_ref[...] = _dot(x_ref[...].astype(BF16), w_ref[...])


def _matmul(x, w, tm, tn=512):
    n, k = x.shape
    m = w.shape[1]
    return pl.pallas_call(
        _mm_kernel,
        grid=(n // tm, m // tn),
        in_specs=[pl.BlockSpec((tm, k), lambda i, j: (i, 0)),
                  pl.BlockSpec((k, tn), lambda i, j: (0, j))],
        out_specs=pl.BlockSpec((tm, tn), lambda i, j: (i, j)),
        out_shape=jax.ShapeDtypeStruct((n, m), F32),
        compiler_params=_cparams("parallel", "arbitrary"),
        name="matmul",
    )(x, w)


def _mm_res_ln_kernel(x_ref, w_ref, r_ref, g_ref, b_ref, o_ref):
    y = _dot(x_ref[...].astype(BF16), w_ref[...])
    o_ref[...] = _ln(ALPHA * r_ref[...] + y, g_ref[...], b_ref[...])


def _matmul_res_ln(x, w, res, g, b, tm):
    n = x.shape[0]
    return pl.pallas_call(
        _mm_res_ln_kernel,
        grid=(n // tm,),
        in_specs=[pl.BlockSpec((tm, D_MODEL), lambda i: (i, 0)),
                  pl.BlockSpec((D_MODEL, D_MODEL), lambda i: (0, 0)),
                  pl.BlockSpec((tm, D_MODEL), lambda i: (i, 0)),
                  pl.BlockSpec((1, D_MODEL), lambda i: (0, 0)),
                  pl.BlockSpec((1, D_MODEL), lambda i: (0, 0))],
        out_specs=pl.BlockSpec((tm, D_MODEL), lambda i: (i, 0)),
        out_shape=jax.ShapeDtypeStruct((n, D_MODEL), F32),
        compiler_params=_cparams("parallel"),
        name="proj_res_ln",
    )(x, w, res, g, b)


def _ret_gammas():
    return [1.0 - 2.0 ** (-5.0 - h) for h in range(RET_HEADS)]


def _ret_kernel(qk_ref, v_ref, s0_ref, din_ref, dq_ref, dk_ref, o_ref, so_ref, r_scr, *, nchunk, dc):
    c = RET_CHUNK

    @pl.when(pl.program_id(1) == 0)
    def _():
        r_scr[...] = s0_ref[0]

    def chunk(ci, carry):
        row = pl.multiple_of(ci * c, c)
        qk = qk_ref[0, pl.ds(row, c), :]
        v = v_ref[0, pl.ds(row, c), :]
        for h in range(RET_HEADS):
            q = qk[:, h * RET_DK:(h + 1) * RET_DK] * (RET_DK ** -0.5)
            k = qk[:, 256 + h * RET_DK:256 + (h + 1) * RET_DK]
            vh = v[:, h * RET_DV:(h + 1) * RET_DV].astype(BF16)
            r = r_scr[h]
            s = _dot_nt(q.astype(BF16), k.astype(BF16)) * din_ref[h]
            o = _dot(s.astype(BF16), vh) + _dot((q * dq_ref[h]).astype(BF16), r.astype(BF16))
            r_scr[h] = r * dc[h] + _dot_tn((k * dk_ref[h]).astype(BF16), vh)
            o_ref[0, pl.ds(row, c), h * RET_DV:(h + 1) * RET_DV] = o
        return carry

    lax.fori_loop(0, nchunk, chunk, 0)

    @pl.when(pl.program_id(1) == pl.num_programs(1) - 1)
    def _():
        so_ref[0] = r_scr[...]


def _retention_prompt(proj3, s0, tt=1024):
    b, t, _ = proj3.shape
    c = RET_CHUNK
    gam = np.array(_ret_gammas(), np.float64)
    i = np.arange(c, dtype=np.float64)
    diff = i[:, None] - i[None, :]
    din = np.where(diff >= 0, gam[:, None, None] ** np.maximum(diff, 0.0), 0.0)
    dq = np.broadcast_to((gam[:, None] ** (i + 1.0))[:, :, None], (RET_HEADS, c, RET_DK))
    dk = np.broadcast_to((gam[:, None] ** (c - 1.0 - i))[:, :, None], (RET_HEADS, c, RET_DK))
    dc = tuple(float(g ** c) for g in gam)
    const = lambda shape: pl.BlockSpec(shape, lambda bi, ti: (0,) * len(shape))
    return pl.pallas_call(
        functools.partial(_ret_kernel, nchunk=tt // c, dc=dc),
        grid=(b, t // tt),
        in_specs=[
            pl.BlockSpec((1, tt, 512), lambda bi, ti: (bi, ti, COL_AQK)),
            pl.BlockSpec((1, tt, 512), lambda bi, ti: (bi, ti, COL_AV)),
            pl.BlockSpec((1, RET_HEADS, RET_DK, RET_DV), lambda bi, ti: (bi, 0, 0, 0)),
            const((RET_HEADS, c, c)), const((RET_HEADS, c, RET_DK)), const((RET_HEADS, c, RET_DK)),
        ],
        out_specs=[
            pl.BlockSpec((1, tt, 512), lambda bi, ti: (bi, ti, 0)),
            pl.BlockSpec((1, RET_HEADS, RET_DK, RET_DV), lambda bi, ti: (bi, 0, 0, 0)),
        ],
        out_shape=[jax.ShapeDtypeStruct((b, t, BRANCH_W), F32),
                   jax.ShapeDtypeStruct((b, RET_HEADS, RET_DK, RET_DV), F32)],
        scratch_shapes=[pltpu.VMEM((RET_HEADS, RET_DK, RET_DV), F32)],
        compiler_params=_cparams("parallel", "arbitrary"),
        name="retention_chunked",
    )(proj3, proj3, s0, jnp.asarray(din, F32), jnp.asarray(dq, F32), jnp.asarray(dk, F32))


def _hgrn_log_f(z, lb):
    log_sig = jnp.minimum(z, 0.0) - jnp.log1p(jnp.exp(-jnp.abs(z)))
    return log_sig + jnp.log1p(lb * jnp.exp(jnp.minimum(-z, EXP_CLIP)))


def _cumsum_rows(g):
    n = g.shape[0]
    row = lax.broadcasted_iota(jnp.int32, g.shape, 0)
    sh = 1
    while sh < n:
        g = g + jnp.where(row >= sh, pltpu.roll(g, sh, 0), 0.0)
        sh *= 2
    return g


def _hgrn_kernel(q_ref, f_ref, i_ref, lb_ref, s0_ref, o_ref, so_ref, s_scr, *, nchunk):
    c = HG_CHUNK
    nslab = c // 8

    @pl.when(pl.program_id(2) == 0)
    def _():
        s_scr[...] = s0_ref[0, 0]

    lb = lb_ref[...]
    sub = 16
    rowi = lax.broadcasted_iota(jnp.int32, (8, HG_DK), 0)
    rowc = lax.broadcasted_iota(jnp.int32, (c, HG_DK), 0)

    def chunk(ci, carry):
        row = pl.multiple_of(ci * c, c)
        qh = _silu(q_ref[0, pl.ds(row, c), :]) * (HG_DK ** -0.5)
        g = _hgrn_log_f(f_ref[0, pl.ds(row, c), :], lb)
        v = i_ref[0, pl.ds(row, c), :]
        b = _cumsum_rows(g)
        ki = 1.0 - jnp.exp(g)
        s_mat = s_scr[...]
        vb = v.astype(BF16)
        o = _dot((qh * jnp.exp(b)).astype(BF16), s_mat.astype(BF16))
        a_blocks = [jnp.zeros((sub, c), F32)]
        for blk in range(1, c // sub):
            lo = blk * sub
            r = b[lo - 1:lo]
            qt = qh[lo:lo + sub] * jnp.exp(b[lo:lo + sub] - r)
            kt = jnp.where(rowc < lo, ki * jnp.exp(jnp.minimum(r - b, 0.0)), 0.0)
            a_blocks.append(_dot_nt(qt.astype(BF16), kt.astype(BF16)))
        o = o + _dot(jnp.concatenate(a_blocks, axis=0).astype(BF16), vb)
        o_sl = [o[8 * j:8 * j + 8] for j in range(nslab)]
        b_sl = [b[8 * j:8 * j + 8] for j in range(nslab)]
        q_sl = [qh[8 * j:8 * j + 8] for j in range(nslab)]
        for s in range(c):
            js = s // 8
            bs, ks, vs = b[s:s + 1], ki[s:s + 1], v[s:s + 1]
            for j in range(js, (s // sub + 1) * (sub // 8)):
                d = b_sl[j] - bs
                if j == js:
                    d = jnp.where(rowi >= s % 8, d, NEG)
                a = jnp.sum(q_sl[j] * ks * jnp.exp(d), axis=-1, keepdims=True)
                o_sl[j] = o_sl[j] + a * vs
        o_ref[0, pl.ds(row, c), :] = jnp.concatenate(o_sl, axis=0)
        b_end = b[c - 1:c]
        decay_col = jnp.broadcast_to(jnp.exp(b_end), (HG_DK, HG_DK)).T
        s_scr[...] = s_mat * decay_col + _dot_tn((ki * jnp.exp(b_end - b)).astype(BF16), v.astype(BF16))
        return carry

    lax.fori_loop(0, nchunk, chunk, 0)

    @pl.when(pl.program_id(2) == pl.num_programs(2) - 1)
    def _():
        so_ref[0, 0] = s_scr[...]


def _hgrn_prompt(proj3, lb, s0, tt=1024):
    b, t, _ = proj3.shape
    col = lambda base: (lambda bi, h, ti: (bi, ti, base * 4 + h))
    return pl.pallas_call(
        functools.partial(_hgrn_kernel, nchunk=tt // HG_CHUNK),
        grid=(b, HG_HEADS, t // tt),
        in_specs=[
            pl.BlockSpec((1, tt, 128), col(COL_BQ)),
            pl.BlockSpec((1, tt, 128), col(COL_BF)),
            pl.BlockSpec((1, tt, 128), col(COL_BI)),
            pl.BlockSpec((1, 128), lambda bi, h, ti: (0, h)),
            pl.BlockSpec((1, 1, HG_DK, HG_DV), lambda bi, h, ti: (bi, h, 0, 0)),
        ],
        out_specs=[
            pl.BlockSpec((1, tt, 128), lambda bi, h, ti: (bi, ti, h)),
            pl.BlockSpec((1, 1, HG_DK, HG_DV), lambda bi, h, ti: (bi, h, 0, 0)),
        ],
        out_shape=[jax.ShapeDtypeStruct((b, t, BRANCH_W), F32),
                   jax.ShapeDtypeStruct((b, HG_HEADS, HG_DK, HG_DV), F32)],
        scratch_shapes=[pltpu.VMEM((HG_DK, HG_DV), F32)],
        compiler_params=_cparams("parallel", "parallel", "arbitrary"),
        name="hgrn2_chunked",
    )(proj3, proj3, proj3, lb, s0)


def _dil_kernel(q_ref, kc_ref, kp_ref, vc_ref, vp_ref, o_ref, lse_ref, s_scr, p_scr):
    i = pl.program_id(2)
    qb = DIL_QB
    npair = DIL_HEADS // 2
    pairs = [slice(c * 128, (c + 1) * 128) for c in range(npair)]
    first = lax.broadcasted_iota(jnp.int32, (qb, 128), 1) < DIL_DH
    t = lax.broadcasted_iota(jnp.int32, (qb, qb), 0)
    s = lax.broadcasted_iota(jnp.int32, (qb, qb), 1)
    for u in range(q_ref.shape[0] // qb):
        rows = slice(u * qb, (u + 1) * qb)
        if u == 0:
            k_prev, v_prev, has_prev = kp_ref, vp_ref, i > 0
        else:
            prev_rows = slice((u - 1) * qb, u * qb)
            k_prev, v_prev, has_prev = kc_ref.at[prev_rows], vc_ref.at[prev_rows], True
        for c, psl in enumerate(pairs):
            q = q_ref[rows, psl] * (DIL_DH ** -0.5)
            kp = k_prev[:, psl].astype(BF16)
            kc = kc_ref[rows, psl].astype(BF16)
            for half in range(2):
                qm = jnp.where(first if half == 0 else jnp.logical_not(first), q, 0.0).astype(BF16)
                s_scr[2 * c + half, :, 0:qb] = _dot_nt(qm, kp)
                s_scr[2 * c + half, :, qb:2 * qb] = _dot_nt(qm, kc)
        in_prev = jnp.logical_and(s >= t, has_prev)
        in_cur = s <= t
        for c, psl in enumerate(pairs):
            lses = []
            for half in range(2):
                h = 2 * c + half
                sp = jnp.where(in_prev, s_scr[h, :, 0:qb], NEG)
                sc = jnp.where(in_cur, s_scr[h, :, qb:2 * qb], NEG)
                m = jnp.max(jnp.maximum(sp, sc), -1, keepdims=True)
                pp = jnp.exp(sp - m)
                pc = jnp.exp(sc - m)
                l = jnp.sum(pp + pc, -1, keepdims=True)
                p_scr[h, :, 0:qb] = (pp / l).astype(BF16)
                p_scr[h, :, qb:2 * qb] = (pc / l).astype(BF16)
                lses.append(m + jnp.log(l))
            lse_ref[rows, psl] = jnp.where(first, lses[0], lses[1])
        for c, psl in enumerate(pairs):
            vp = v_prev[:, psl].astype(BF16)
            vc = vc_ref[rows, psl].astype(BF16)
            o2 = [_dot(p_scr[2 * c + half, :, 0:qb], vp) + _dot(p_scr[2 * c + half, :, qb:2 * qb], vc)
                  for half in range(2)]
            o_ref[rows, psl] = jnp.where(first, o2[0], o2[1])


def _dilated_prompt(qkv4, col0, g):
    b, dil, tr, _ = qkv4.shape
    step = min(4 * DIL_QB, tr)
    nsub = step // DIL_QB
    assert dil == DIL_GROUPS[g][1] and tr % step == 0 and step % DIL_QB == 0
    cur = lambda c: (lambda bi, r, i: (bi, r, i, c))
    prev = lambda c: (lambda bi, r, i: (bi, r, jnp.maximum(i * nsub - 1, 0), c))
    blk = (None, None, step, 512)
    pblk = (None, None, DIL_QB, 512)
    oblk = (None, step, 512)
    o, lse = pl.pallas_call(
        _dil_kernel,
        grid=(b, dil, tr // step),
        in_specs=[pl.BlockSpec(blk, cur(col0)),
                  pl.BlockSpec(blk, cur(col0 + 1)), pl.BlockSpec(pblk, prev(col0 + 1)),
                  pl.BlockSpec(blk, cur(col0 + 2)), pl.BlockSpec(pblk, prev(col0 + 2))],
        out_specs=[pl.BlockSpec(oblk, lambda bi, r, i: (bi, i, r)),
                   pl.BlockSpec(oblk, lambda bi, r, i: (bi, i, r))],
        out_shape=[jax.ShapeDtypeStruct((b, tr, dil * 512), F32),
                   jax.ShapeDtypeStruct((b, tr, dil * 512), F32)],
        scratch_shapes=[pltpu.VMEM((DIL_HEADS, DIL_QB, 2 * DIL_QB), F32),
                        pltpu.VMEM((DIL_HEADS, DIL_QB, 2 * DIL_QB), BF16)],
        compiler_params=_cparams("parallel", "parallel", "arbitrary"),
        name="dilated_attn_%d" % g,
    )(qkv4, qkv4, qkv4, qkv4, qkv4)
    return o.reshape(b * tr * dil, 512), lse.reshape(b * tr * dil, 512)


def _head_norm(x, centre):
    parts = []
    for h in range(4):
        xh = x[:, h * 128:(h + 1) * 128]
        if centre:
            xh = xh - jnp.mean(xh, -1, keepdims=True)
        parts.append(xh * lax.rsqrt(jnp.mean(xh * xh, -1, keepdims=True) + EPS))
    return jnp.concatenate(parts, axis=1)


def _merge_kernel(*refs, n_groups):
    h_ref, hb_ref, oa_ref, ag_ref, ob_ref, bg_ref = refs[:6]
    c_refs = refs[6:6 + (1 if n_groups == 1 else 2 * n_groups)]
    wgate_ref, gn_ref, hn_ref, wb_ref, wo_ref, lg_ref, lb_ref, o_ref = refs[6 + len(c_refs):]
    if n_groups == 1:
        oc = c_refs[0][...]
    else:
        ls = [r[...] for r in c_refs[n_groups:]]
        m = functools.reduce(jnp.maximum, ls)
        es = [jnp.exp(x - m) for x in ls]
        den = functools.reduce(lambda a, b: a + b, es)
        oc = functools.reduce(lambda a, b: a + b, [(e / den) * r[...] for e, r in zip(es, c_refs[:n_groups])])
    oa = _head_norm(oa_ref[...], True) * gn_ref[...] * _silu(ag_ref[...])
    ob = _head_norm(ob_ref[...], False) * hn_ref[...] * jax.nn.sigmoid(bg_ref[...])
    hb = hb_ref[...]
    merged = None
    for br, x in enumerate((oa, ob, oc)):
        gate = jax.nn.sigmoid(_dot(hb, wgate_ref[:, br * D_MODEL:(br + 1) * D_MODEL]))
        term = gate * _dot(x.astype(BF16), wb_ref[br])
        merged = term if merged is None else merged + term
    mix = _dot(merged.astype(BF16), wo_ref[...])
    o_ref[...] = _ln(ALPHA * h_ref[...] + mix, lg_ref[...], lb_ref[...])


def _merge(h, hb, proj, oa, ob, c_parts, wgate, gn, hn, wb, wo, lg, lb, tm=256):
    n = h.shape[0]
    n_groups = 1 if len(c_parts) == 1 else len(c_parts) // 2
    row512 = lambda cb: pl.BlockSpec((tm, 512), lambda i: (i, cb))
    row1024 = pl.BlockSpec((tm, D_MODEL), lambda i: (i, 0))
    const = lambda shape: pl.BlockSpec(shape, lambda i: (0,) * len(shape))
    return pl.pallas_call(
        functools.partial(_merge_kernel, n_groups=n_groups),
        grid=(n // tm,),
        in_specs=[row1024, row1024, row512(0), row512(COL_AG), row512(0), row512(COL_BG)]
                 + [row512(0)] * len(c_parts)
                 + [const((D_MODEL, 3 * D_MODEL)), const((1, 512)), const((1, 512)),
                    const((3, 512, D_MODEL)), const((D_MODEL, D_MODEL)),
                    const((1, D_MODEL)), const((1, D_MODEL))],
        out_specs=row1024,
        out_shape=jax.ShapeDtypeStruct((n, D_MODEL), F32),
        compiler_params=_cparams("parallel"),
        name="branch_merge",
    )(h, hb, oa, proj, ob, proj, *c_parts, wgate, gn, hn, wb, wo, lg, lb)


def _xattn_block_kernel(h_ref, wq_ref, k_ref, v_ref, wo_ref, g_ref, b_ref, o_ref):
    h = h_ref[0]
    q = _dot(h.astype(BF16), wq_ref[...])
    outs = []
    for hd in range(X_HEADS):
        sl = slice(hd * X_DH, (hd + 1) * X_DH)
        s = _dot_nt(q[:, sl].astype(BF16), k_ref[0, :, sl]) * (X_DH ** -0.5)
        m = jnp.max(s, -1, keepdims=True)
        p = jnp.exp(s - m)
        p = p / jnp.sum(p, -1, keepdims=True)
        outs.append(_dot(p.astype(BF16), v_ref[0, :, sl]).astype(BF16))
    y = _dot(jnp.concatenate(outs, axis=1), wo_ref[...])
    o_ref[0] = _ln(ALPHA * h + y, g_ref[...], b_ref[...])


def _xattn_block(h3, wq, mk, mv, wo, g, b, tq=512):
    bsz, t, _ = h3.shape
    row = pl.BlockSpec((1, tq, D_MODEL), lambda bi, ti: (bi, ti, 0))
    mem = pl.BlockSpec((1, MEM_LEN, D_MODEL), lambda bi, ti: (bi, 0, 0))
    wspec = pl.BlockSpec((D_MODEL, D_MODEL), lambda bi, ti: (0, 0))
    vec = pl.BlockSpec((1, D_MODEL), lambda bi, ti: (0, 0))
    return pl.pallas_call(
        _xattn_block_kernel,
        grid=(bsz, t // tq),
        in_specs=[row, wspec, mem, mem, wspec, vec, vec],
        out_specs=row,
        out_shape=jax.ShapeDtypeStruct((bsz, t, D_MODEL), F32),
        compiler_params=_cparams("parallel", "arbitrary"),
        name="cross_attn_block",
    )(h3, wq, mk, mv, wo, g, b)


def _xattn_s_kernel(q_ref, k_ref, v_ref, o_ref, *, bb):
    nq = q_ref.shape[1]
    nk = MEM_LEN * X_HEADS
    same_head = (lax.broadcasted_iota(jnp.int32, (nq, nk), 0) % X_HEADS
                 == lax.broadcasted_iota(jnp.int32, (nq, nk), 1) % X_HEADS)
    for bi in range(bb):
        k2 = k_ref[bi].reshape(nk, X_DH).astype(BF16)
        v2 = v_ref[bi].reshape(nk, X_DH).astype(BF16)
        s = _dot_nt(q_ref[bi].astype(BF16), k2) * (X_DH ** -0.5)
        s = jnp.where(same_head, s, NEG)
        m = jnp.max(s, -1, keepdims=True)
        p = jnp.exp(s - m)
        p = p / jnp.sum(p, -1, keepdims=True)
        o_ref[bi] = _dot(p.astype(BF16), v2)


def _xattn_sample(q3, mk5, mv5, layer, bb=2):
    b, nq, _ = q3.shape
    mem_spec = pl.BlockSpec((None, bb, MEM_LEN, X_HEADS, X_DH), lambda bi: (layer, bi, 0, 0, 0))
    row = pl.BlockSpec((bb, nq, X_DH), lambda bi: (bi, 0, 0))
    return pl.pallas_call(
        functools.partial(_xattn_s_kernel, bb=bb),
        grid=(b // bb,),
        in_specs=[row, mem_spec, mem_spec],
        out_specs=row,
        out_shape=jax.ShapeDtypeStruct((b, nq, X_DH), F32),
        compiler_params=_cparams("parallel"),
        name="cross_attn_step",
    )(q3, mk5, mv5)


def _pick_col(tile, onehot):
    return jnp.sum(jnp.where(onehot, tile, 0.0), axis=1, keepdims=True)


def _pick_row(slab, sub_hit):
    return jnp.sum(jnp.where(sub_hit, slab, 0.0), axis=0, keepdims=True)


def _ret_s_kernel(q_ref, k_ref, v_ref, s_ref, o_ref, so_ref, qt_scr, kt_scr, *, nt, bb):
    j = pl.program_id(0)
    nb = q_ref.shape[1]

    @pl.when(j == 0)
    def _():
        for t in range(nt):
            for c in range(2):
                sl = slice(c * 128, (c + 1) * 128)
                qt_scr[t, sl, :] = (q_ref[t, :, sl] * (RET_DK ** -0.5)).T
                kt_scr[t, sl, :] = k_ref[t, :, sl].T

    lane = lax.broadcasted_iota(jnp.int32, (RET_DK, nb), 1)
    sub = lax.broadcasted_iota(jnp.int32, (8, 128), 0)
    o_ref[...] = jnp.zeros_like(o_ref)
    gam = _ret_gammas()

    def body(bi, carry):
        bg = j * bb + bi
        onehot = lane == bg
        row_g = pl.multiple_of((bg // 8) * 8, 8)
        row_l = pl.multiple_of((bi // 8) * 8, 8)
        sub_hit = sub == bi % 8
        for h in range(RET_HEADS):
            r = s_ref[bi, h]
            for t in range(nt):
                ksl = slice(h * RET_DK, (h + 1) * RET_DK)
                qc = _pick_col(qt_scr[t, ksl, :], onehot)
                kc = _pick_col(kt_scr[t, ksl, :], onehot)
                vsl = slice(h * RET_DV, (h + 1) * RET_DV)
                vrow = _pick_row(v_ref[t, pl.ds(row_g, 8), vsl], sub_hit)
                r = r * gam[h] + kc * vrow
                orow = jnp.sum(r * qc, axis=0, keepdims=True)
                o_ref[t, pl.ds(row_l, 8), vsl] = jnp.where(sub_hit, orow, o_ref[t, pl.ds(row_l, 8), vsl])
            so_ref[bi, h] = r
        return carry

    lax.fori_loop(0, bb, body, 0)


def _retention_sample(q_t, k_t, v_t, state, layer, bb=16):
    nt, nb, _ = q_t.shape
    full = lambda w: pl.BlockSpec((nt, nb, w), lambda j: (0, 0, 0))
    return pl.pallas_call(
        functools.partial(_ret_s_kernel, nt=nt, bb=bb),
        grid=(nb // bb,),
        in_specs=[full(256), full(256), full(512),
                  pl.BlockSpec((None, bb, RET_HEADS, RET_DK, RET_DV), lambda j: (layer, j, 0, 0, 0))],
        out_specs=[pl.BlockSpec((nt, bb, 512), lambda j: (0, j, 0)),
                   pl.BlockSpec((bb, RET_HEADS, RET_DK, RET_DV), lambda j: (j, 0, 0, 0))],
        out_shape=[jax.ShapeDtypeStruct((nt, nb, 512), F32),
                   jax.ShapeDtypeStruct((nb, RET_HEADS, RET_DK, RET_DV), F32)],
        scratch_shapes=[pltpu.VMEM((nt, 256, nb), F32), pltpu.VMEM((nt, 256, nb), F32)],
        compiler_params=_cparams("arbitrary"),
        name="retention_step",
    )(q_t, k_t, v_t, state)


def _hgrn_s_kernel(q_ref, f_ref, i_ref, lb_ref, s_ref, o_ref, so_ref, qt_scr, ft_scr, kt_scr, *, nt, bb):
    j = pl.program_id(0)
    nb = q_ref.shape[1]

    @pl.when(j == 0)
    def _():
        for t in range(nt):
            for h in range(HG_HEADS):
                sl = slice(h * HG_DK, (h + 1) * HG_DK)
                f = jnp.exp(_hgrn_log_f(f_ref[t, :, sl], lb_ref[:, sl]))
                qt_scr[t, sl, :] = (_silu(q_ref[t, :, sl]) * (HG_DK ** -0.5)).T
                ft_scr[t, sl, :] = f.T
                kt_scr[t, sl, :] = (1.0 - f).T

    lane = lax.broadcasted_iota(jnp.int32, (HG_DK, nb), 1)
    sub = lax.broadcasted_iota(jnp.int32, (8, 128), 0)
    o_ref[...] = jnp.zeros_like(o_ref)

    def body(bi, carry):
        bg = j * bb + bi
        onehot = lane == bg
        row_g = pl.multiple_of((bg // 8) * 8, 8)
        row_l = pl.multiple_of((bi // 8) * 8, 8)
        sub_hit = sub == bi % 8
        for h in range(HG_HEADS):
            sl = slice(h * HG_DK, (h + 1) * HG_DK)
            s_mat = s_ref[bi, h]
            for t in range(nt):
                qc = _pick_col(qt_scr[t, sl, :], onehot)
                fc = _pick_col(ft_scr[t, sl, :], onehot)
                kc = _pick_col(kt_scr[t, sl, :], onehot)
                vrow = _pick_row(i_ref[t, pl.ds(row_g, 8), sl], sub_hit)
                s_mat = s_mat * fc + kc * vrow
                orow = jnp.sum(s_mat * qc, axis=0, keepdims=True)
                o_ref[t, pl.ds(row_l, 8), sl] = jnp.where(sub_hit, orow, o_ref[t, pl.ds(row_l, 8), sl])
            so_ref[bi, h] = s_mat
        return carry

    lax.fori_loop(0, bb, body, 0)


def _hgrn_sample(q_t, f_t, i_t, lb, state, layer, bb=16):
    nt, nb, _ = q_t.shape
    full = pl.BlockSpec((nt, nb, 512), lambda j: (0, 0, 0))
    return pl.pallas_call(
        functools.partial(_hgrn_s_kernel, nt=nt, bb=bb),
        grid=(nb // bb,),
        in_specs=[full, full, full, pl.BlockSpec((1, 512), lambda j: (0, 0)),
                  pl.BlockSpec((None, bb, HG_HEADS, HG_DK, HG_DV), lambda j: (layer, j, 0, 0, 0))],
        out_specs=[pl.BlockSpec((nt, bb, 512), lambda j: (0, j, 0)),
                   pl.BlockSpec((bb, HG_HEADS, HG_DK, HG_DV), lambda j: (j, 0, 0, 0))],
        out_shape=[jax.ShapeDtypeStruct((nt, nb, 512), F32),
                   jax.ShapeDtypeStruct((nb, HG_HEADS, HG_DK, HG_DV), F32)],
        scratch_shapes=[pltpu.VMEM((nt, 512, nb), F32)] * 3,
        compiler_params=_cparams("arbitrary"),
        name="hgrn2_step",
    )(q_t, f_t, i_t, lb, state)


def _dil_s_kernel(new_ref, k0_ref, v0_ref, k1_ref, v1_ref, k2_ref, v2_ref, o_ref, qpad_scr, opad_scr, *, nt):
    caches = ((k0_ref, v0_ref), (k1_ref, v1_ref), (k2_ref, v2_ref))
    scale = DIL_DH ** -0.5
    lane = lax.broadcasted_iota(jnp.int32, (DIL_DH, 128), 1)
    sub8 = lax.broadcasted_iota(jnp.int32, (DIL_HEADS, 128), 0)
    lane8 = lax.broadcasted_iota(jnp.int32, (DIL_HEADS, 128), 1)
    col_id = lambda g, i, h: (g * nt + i) * DIL_HEADS + h

    qpad_scr[...] = jnp.zeros_like(qpad_scr)
    opad_scr[...] = jnp.zeros_like(opad_scr)
    for g in range(N_GROUPS):
        for i in range(nt):
            r0 = col_id(g, i, 0)
            qpad_scr[r0:r0 + DIL_HEADS, 0:DIL_DH] = new_ref[0, i, g]
    qt = qpad_scr[...].T[0:DIL_DH]
    qcol = lambda g, i, h: qt[:, col_id(g, i, h):col_id(g, i, h) + 1]

    oc = jnp.zeros((DIL_DH, 128), F32)
    m_all = [[None] * nt for _ in range(N_GROUPS)]
    l_all = [[None] * nt for _ in range(N_GROUPS)]
    pn_all = [[None] * nt for _ in range(N_GROUPS)]
    for g in range(N_GROUPS):
        kc_ref, vc_ref = caches[g]
        dil = DIL_GROUPS[g][1]
        ntile = kc_ref.shape[-1] // 128
        q_t = [new_ref[0, i, g] for i in range(nt)]
        kn = [new_ref[0, j, 3 + g] for j in range(nt)]
        new_score = lambda i, j: jnp.sum(q_t[i] * kn[j], axis=-1, keepdims=True) * scale
        if dil == 1:
            for i in range(nt):
                s = jnp.zeros((DIL_HEADS, 128), F32)
                for h in range(DIL_HEADS):
                    row = jnp.sum(kc_ref[0, h] * qcol(g, i, h), axis=0, keepdims=True)
                    s = jnp.where(sub8 == h, row, s)
                s = jnp.where(lane8 >= i, s * scale, NEG)
                sn = [new_score(i, j) for j in range(i + 1)]
                m = jnp.max(s, -1, keepdims=True)
                for x in sn:
                    m = jnp.maximum(m, x)
                p = jnp.exp(s - m)
                pn = [jnp.exp(x - m) for x in sn]
                l = jnp.sum(p, -1, keepdims=True)
                for x in pn:
                    l = l + x
                for h in range(DIL_HEADS):
                    col = jnp.sum(vc_ref[0, h] * p[h:h + 1, :], axis=1, keepdims=True)
                    oc = jnp.where(lane == col_id(g, i, h), col, oc)
                m_all[g][i], l_all[g][i], pn_all[g][i] = m, l, list(zip(pn, range(i + 1)))
        else:
            assert nt <= dil and 128 % dil == 0
            cls8 = [(lane8 & (dil - 1)) == i for i in range(nt)]
            cls = [(lane & (dil - 1)) == i for i in range(nt)]
            s_t = [jnp.zeros((DIL_HEADS, 128), F32) for _ in range(ntile)]
            for h in range(DIL_HEADS):
                qsel = jnp.zeros((DIL_DH, 128), F32)
                for i in range(nt):
                    qsel = jnp.where(cls[i], qcol(g, i, h), qsel)
                for t in range(ntile):
                    row = jnp.sum(kc_ref[0, h, :, t * 128:(t + 1) * 128] * qsel, axis=0, keepdims=True)
                    s_t[t] = jnp.where(sub8 == h, row, s_t[t])
            s_t = [s * scale for s in s_t]
            smax = s_t[0]
            for t in range(1, ntile):
                smax = jnp.maximum(smax, s_t[t])
            m_tile = jnp.zeros((DIL_HEADS, 128), F32)
            owned = cls8[0]
            for i in range(nt):
                sn = new_score(i, i)
                m = jnp.maximum(jnp.max(jnp.where(cls8[i], smax, NEG), -1, keepdims=True), sn)
                m_all[g][i] = m
                pn_all[g][i] = [(jnp.exp(sn - m), i)]
                m_tile = jnp.where(cls8[i], m, m_tile)
                owned = jnp.logical_or(owned, cls8[i])
            p_t = [jnp.exp(jnp.where(owned, s - m_tile, NEG)) for s in s_t]
            psum = p_t[0]
            for t in range(1, ntile):
                psum = psum + p_t[t]
            for i in range(nt):
                l_all[g][i] = jnp.sum(jnp.where(cls8[i], psum, 0.0), -1, keepdims=True) + pn_all[g][i][0][0]
            for h in range(DIL_HEADS):
                acc = jnp.zeros((DIL_DH, 128), F32)
                for t in range(ntile):
                    acc = acc + vc_ref[0, h, :, t * 128:(t + 1) * 128] * p_t[t][h:h + 1, :]
                for i in range(nt):
                    col = jnp.sum(jnp.where(cls[i], acc, 0.0), axis=1, keepdims=True)
                    oc = jnp.where(lane == col_id(g, i, h), col, oc)

    opad_scr[0:DIL_DH, :] = oc
    ot = opad_scr[...].T
    for i in range(nt):
        outs, lses = [], []
        for g in range(N_GROUPS):
            r0 = col_id(g, i, 0)
            o = ot[r0:r0 + DIL_HEADS, 0:DIL_DH]
            for pj, j in pn_all[g][i]:
                o = o + pj * new_ref[0, j, 6 + g]
            outs.append(o / l_all[g][i])
            lses.append(m_all[g][i] + jnp.log(l_all[g][i]))
        m = jnp.maximum(jnp.maximum(lses[0], lses[1]), lses[2])
        es = [jnp.exp(x - m) for x in lses]
        den = es[0] + es[1] + es[2]
        o_ref[0, i] = (es[0] / den) * outs[0] + (es[1] / den) * outs[1] + (es[2] / den) * outs[2]


def _dilated_sample(new9, caches, layer):
    nb, nt = new9.shape[:2]
    views, specs = [], []
    for g, (kbuf, vbuf) in enumerate(caches):
        w = kbuf.shape[2]
        assert w == DIL_GROUPS[g][0] and w // DIL_GROUPS[g][1] == 128
        for buf in (kbuf, vbuf):
            views.append(jnp.transpose(buf, (0, 1, 3, 4, 2)))
            specs.append(pl.BlockSpec((None, 1, DIL_HEADS, DIL_DH, w), lambda j: (layer, j, 0, 0, 0)))
    return pl.pallas_call(
        functools.partial(_dil_s_kernel, nt=nt),
        grid=(nb,),
        in_specs=[pl.BlockSpec((1, nt, 9, DIL_HEADS, DIL_DH), lambda j: (j, 0, 0, 0, 0))] + specs,
        out_specs=pl.BlockSpec((1, nt, DIL_HEADS, DIL_DH), lambda j: (j, 0, 0, 0)),
        out_shape=jax.ShapeDtypeStruct((nb, nt, DIL_HEADS, DIL_DH), F32),
        scratch_shapes=[pltpu.VMEM((128, 128), F32), pltpu.VMEM((128, 128), F32)],
        compiler_params=_cparams("parallel"),
        name="dilated_attn_step",
    )(new9, *views)


def _rotary_tables(pos):
    pos = jnp.asarray(np.asarray(pos), jnp.int32)
    lane = np.arange(128) % 64
    out = []
    for rot_dim, theta in ((RET_DK, RET_THETA), (ROPE_DIM, ROPE_THETA)):
        half = rot_dim // 2
        inv = jnp.power(jnp.float32(theta), -jnp.arange(half, dtype=F32) / half)
        ang = pos.astype(F32)[:, None] * inv[None, :]
        idx = lane % half
        cos = jnp.where(lane < rot_dim, jnp.cos(ang)[:, idx], 1.0)
        sin = jnp.sin(ang)[:, idx]
        out += [cos, jnp.where(lane < half, -sin, 0.0),
                jnp.where((lane >= half) & (lane < rot_dim), sin, 0.0)]
    return jnp.stack(out, 0).astype(F32)


def _hgrn_lower_bounds(raw):
    p = jax.nn.softmax(raw.astype(F32), axis=0)
    return jnp.cumsum(p, axis=0) - p[0:1]


def _row(v):
    return v.reshape(1, -1)


def kernel(x_prompt, x_sample, mem_prompt, state_ret, state_hgrn, cache_win_k0, cache_win_v0, cache_win_k1, cache_win_v1, cache_win_k2, cache_win_v2, cache_mem_k, cache_mem_v, ln_g, ln_b, ffn_w_gate, ffn_w_up, ffn_w_down, w_in, ret_gn_g, hgrn_lb_raw, hgrn_norm_g, w_branch, w_out, xattn_w_q, xattn_w_k, xattn_w_v, xattn_w_o):
    bp, tp, _ = x_prompt.shape
    bs, ts, _ = x_sample.shape
    np_, ns = bp * tp, bs * ts
    lb_all = _hgrn_lower_bounds(hgrn_lb_raw)
    tab_p = _rotary_tables(np.arange(tp))
    tab_res = [tab_p] + [_rotary_tables(np.arange(tp).reshape(tp // dil, dil).T.reshape(-1))
                         for _, dil in DIL_GROUPS[1:]]
    tm_s = ns
    tab_s = _rotary_tables(PAST_LEN + (np.arange(tm_s) % ts))
    caches = ((cache_win_k0, cache_win_v0), (cache_win_k1, cache_win_v1), (cache_win_k2, cache_win_v2))
    mem2 = mem_prompt.reshape(bp * MEM_LEN, D_MODEL)

    xp = x_prompt.reshape(np_, D_MODEL)
    xs = x_sample.reshape(ns, D_MODEL)
    tm_p = 1024
    acc = {k: [] for k in ("ret_p", "hg_p", "mk_p", "mv_p", "ret_s", "hg_s")}
    for g in range(N_GROUPS):
        for k in ("wk%d_p", "wv%d_p", "wk%d_s", "wv%d_s"):
            acc[k % g] = []

    for l in range(DEPTH):
        wg = ffn_w_gate[l].astype(BF16)
        wu = ffn_w_up[l].astype(BF16)
        wd = ffn_w_down[l].astype(BF16)
        win = w_in[l].astype(BF16)
        w_main = jnp.concatenate(
            [win[:, :W_IN_CQ], win[:, W_IN_CQ:W_IN_CQ + 512], win[:, W_IN_CK:W_IN_CK + 512],
             win[:, W_IN_CV:W_IN_CV + 512]], axis=1)
        w_gate = win[:, W_IN_GATE:]
        w_grp = [None] + [jnp.concatenate([win[:, o + g * 512:o + (g + 1) * 512]
                                           for o in (W_IN_CQ, W_IN_CK, W_IN_CV)], axis=1)
                          for g in range(1, N_GROUPS)]
        wb = w_branch[l].astype(BF16)
        wo = w_out[l].astype(BF16)
        wq = xattn_w_q[l].astype(BF16)
        wk = xattn_w_k[l].astype(BF16)
        wv = xattn_w_v[l].astype(BF16)
        wxo = xattn_w_o[l].astype(BF16)
        lng = [_row(ln_g[l, i]) for i in range(4)]
        lnb = [_row(ln_b[l, i]) for i in range(4)]
        gn, hn, lb = _row(ret_gn_g[l]), _row(hgrn_norm_g[l]), _row(lb_all[l])

        mk = _matmul(mem2, wk, tm=min(512, bp * MEM_LEN))
        mv = _matmul(mem2, wv, tm=min(512, bp * MEM_LEN))
        h, hb = _ffn_ln(xp, wg[0], wu[0], wd[0], lng[0], lnb[0])
        proj = _inproj(hb, w_main, tab_p, tm_p, (COL_AQK,), (COL_C0, COL_C0 + 1))
        proj3 = proj.reshape(bp, tp, MAIN_COLS)
        oa, rp = _retention_prompt(proj3, jnp.zeros((bp, RET_HEADS, RET_DK, RET_DV), F32))
        ob, gp = _hgrn_prompt(proj3, lb, jnp.zeros((bp, HG_HEADS, HG_DK, HG_DV), F32))
        dres = [_dilated_prompt(proj.reshape(bp, 1, tp, MAIN_COLS), COL_C0, 0)]
        pgrp = [None]
        for g in range(1, N_GROUPS):
            dil = DIL_GROUPS[g][1]
            h_res = hb.reshape(bp, tp // dil, dil, D_MODEL).transpose(0, 2, 1, 3).reshape(np_, D_MODEL)
            pg = _inproj(h_res, w_grp[g], tab_res[g], tm_p, (), (0, 1)).reshape(bp, dil, tp // dil, 1536)
            pgrp.append(pg)
            dres.append(_dilated_prompt(pg, 0, g))
        h = _merge(h, hb, proj, oa.reshape(np_, 512), ob.reshape(np_, 512),
                   [d[0] for d in dres] + [d[1] for d in dres], w_gate, gn, hn, wb, wo, lng[1], lnb[1])
        h = _xattn_block(h.reshape(bp, tp, D_MODEL), wq, mk.astype(BF16).reshape(bp, MEM_LEN, D_MODEL),
                         mv.astype(BF16).reshape(bp, MEM_LEN, D_MODEL), wxo, lng[2], lnb[2])
        xp, _ = _ffn_ln(h.reshape(np_, D_MODEL), wg[1], wu[1], wd[1], lng[3], lnb[3])

        acc["ret_p"].append(rp)
        acc["hg_p"].append(gp)
        acc["mk_p"].append(mk.reshape(bp, MEM_LEN, X_HEADS, X_DH))
        acc["mv_p"].append(mv.reshape(bp, MEM_LEN, X_HEADS, X_DH))
        for g, (window, dil) in enumerate(DIL_GROUPS):
            keep = min(window, tp)
            assert keep % dil == 0
            for name, off in (("wk%d_p", 1), ("wv%d_p", 2)):
                if g == 0:
                    c0 = (COL_C0 + off) * 512
                    rows = proj3[:, tp - keep:, c0:c0 + 512]
                else:
                    rows = pgrp[g][:, :, (tp - keep) // dil:, off * 512:(off + 1) * 512]
                    rows = rows.transpose(0, 2, 1, 3)
                acc[name % g].append(rows.reshape(bp, keep, DIL_HEADS, DIL_DH))

        h, hb = _ffn_ln(xs, wg[0], wu[0], wd[0], lng[0], lnb[0])
        proj = _inproj(hb, w_main, tab_s, tm_s, (COL_AQK,), (COL_C0, COL_C0 + 1))
        proj3 = proj.reshape(bs, ts, MAIN_COLS)
        tmaj = lambda lo, hi: proj3[:, :, lo:hi].transpose(1, 0, 2)
        oa, rs = _retention_sample(tmaj(0, 256), tmaj(256, 512), tmaj(512, 1024), state_ret, l)
        ob, gs = _hgrn_sample(tmaj(COL_BQ * 512, COL_BF * 512), tmaj(COL_BF * 512, COL_BI * 512),
                              tmaj(COL_BI * 512, COL_BG * 512), lb, state_hgrn, l)
        oa = oa.transpose(1, 0, 2).reshape(ns, 512)
        ob = ob.transpose(1, 0, 2).reshape(ns, 512)
        qkv_s = [proj[:, COL_C0 * 512:(COL_C0 + 3) * 512]]
        qkv_s += [_inproj(hb, w_grp[g], tab_s, tm_s, (), (0, 1)) for g in range(1, N_GROUPS)]
        new9 = jnp.stack([qkv_s[g][:, c * 512:(c + 1) * 512].reshape(bs, ts, DIL_HEADS, DIL_DH)
                          for c in range(3) for g in range(N_GROUPS)], axis=2)
        oc = _dilated_sample(new9, caches, l).reshape(ns, 512)
        h = _merge(h, hb, proj, oa, ob, [oc], w_gate, gn, hn, wb, wo, lng[1], lnb[1])
        q = _matmul(h, wq, tm=tm_s)
        xo = _xattn_sample(q.reshape(bs, ts * X_HEADS, X_DH), cache_mem_k, cache_mem_v, l)
        h = _matmul_res_ln(xo.reshape(ns, D_MODEL), wxo, h, lng[2], lnb[2], tm=tm_s)
        xs, _ = _ffn_ln(h, wg[1], wu[1], wd[1], lng[3], lnb[3])

        acc["ret_s"].append(rs)
        acc["hg_s"].append(gs)
        for g in range(N_GROUPS):
            acc["wk%d_s" % g].append(new9[:, :, 3 + g])
            acc["wv%d_s" % g].append(new9[:, :, 6 + g])

    st = {k: jnp.stack(v, axis=0) for k, v in acc.items()}
    return (xp.reshape(bp, tp, D_MODEL), xs.reshape(bs, ts, D_MODEL),
            st["ret_p"], st["hg_p"],
            st["wk0_p"], st["wv0_p"], st["wk1_p"], st["wv1_p"], st["wk2_p"], st["wv2_p"],
            st["mk_p"], st["mv_p"],
            st["ret_s"], st["hg_s"],
            st["wk0_s"], st["wv0_s"], st["wk1_s"], st["wv1_s"], st["wk2_s"], st["wv2_s"])
```
